```python
import math
import jax, jax.numpy as jnp
from jax import lax
import numpy as np


D_MODEL = 1024
BATCH = 16
SEQ = 2048
DEPTH = 4

GRID_W = 64
Q_BLOCK = 128
EPS = 1e-6

A_HEADS = 8
A_KV_HEADS = 2
A_GROUP = A_HEADS // A_KV_HEADS
A_HEAD_DIM = 64
AXIAL_THETA = 10000.0

B_HEADS = 8
B_QK_DIM = 32
B_V_DIM = 2 * B_QK_DIM
B_ROT_DIM = B_QK_DIM // 4
ROPE_THETA = 500000.0

C_HEADS = 8
C_Q_RANK = 384
C_KV_RANK = 256
C_NOPE_DIM = 64
C_ROPE_DIM = 32
C_V_DIM = 64
MLA_THETA = 10000.0

N_BRANCH = 3
BRANCH_W = 512
D_FF = 4 * D_MODEL

A_Q_W = A_HEADS * A_HEAD_DIM
A_KV_W = A_KV_HEADS * A_HEAD_DIM
B_QK_W = B_HEADS * 2 * B_QK_DIM
B_V_W = B_HEADS * B_V_DIM
GATE_W = N_BRANCH * D_MODEL
IN_SIZES = (A_Q_W, A_KV_W, A_KV_W, B_QK_W, B_QK_W, B_V_W, C_Q_RANK, C_KV_RANK, C_ROPE_DIM, GATE_W)
IN_COLS = A_Q_W + 2 * A_KV_W + 2 * B_QK_W + B_V_W + C_Q_RANK + C_KV_RANK + C_ROPE_DIM + GATE_W

kernel_name = 'hybrid_gated_gqa_diff_mla_encoder'


def _rms_norm(x, g):
    xf = x.astype(jnp.float32)
    y = xf * lax.rsqrt(jnp.mean(xf * xf, axis=-1, keepdims=True) + EPS)
    return (y * g.astype(jnp.float32)).astype(x.dtype)


def _angles(pos, dim, theta):
    inv_freq = theta ** (-jnp.arange(0, dim, 2, dtype=jnp.float32) / dim)
    return pos.astype(jnp.float32)[:, None] * inv_freq[None, :]


def _rotate(x, ang):
    cos = jnp.cos(ang)[:, None, :].astype(x.dtype)
    sin = jnp.sin(ang)[:, None, :].astype(x.dtype)
    x1, x2 = jnp.split(x, 2, axis=-1)
    return jnp.concatenate([x1 * cos - x2 * sin, x2 * cos + x1 * sin], axis=-1)


def _softmax_f32(s):
    return jax.nn.softmax(s.astype(jnp.float32), axis=-1)


def _sweep_query_blocks(block_fn, *qs):
    b, s = qs[0].shape[:2]
    nb = s // Q_BLOCK
    blocked = tuple(jnp.swapaxes(q.reshape((b, nb, Q_BLOCK) + q.shape[2:]), 0, 1) for q in qs)
    out = lax.map(lambda qb: block_fn(*qb), blocked)
    out = jnp.swapaxes(out, 0, 1)
    return out.reshape((b, s) + out.shape[3:])


def _split_columns(p):
    out, start = [], 0
    for n in IN_SIZES:
        out.append(p[..., start:start + n])
        start += n
    return out


def _gqa_axial(q, k, v, q_g, k_g, row_ang, col_ang):
    b, s = q.shape[:2]
    half = A_HEAD_DIM // 2

    def axial(t):
        return jnp.concatenate([_rotate(t[..., :half], row_ang), _rotate(t[..., half:], col_ang)], axis=-1)

    q = axial(_rms_norm(q, q_g)) * (A_HEAD_DIM ** -0.5)
    k = axial(_rms_norm(k, k_g))
    q = q.reshape(b, s, A_KV_HEADS, A_GROUP, A_HEAD_DIM)

    def block(qb):
        sc = jnp.einsum('bqkgd,bskd->bkgqs', qb, k)
        p = _softmax_f32(sc).astype(v.dtype)
        return jnp.einsum('bkgqs,bskd->bqkgd', p, v)

    o = _sweep_query_blocks(block, q)
    return o.reshape(b, s, A_HEADS * A_HEAD_DIM)


def _diff_attention(q, k, v, lam_p, sub_g, lambda_init, rot_ang):
    b, s = q.shape[:2]

    def partial_rot(t):
        t = t.reshape(b, s, B_HEADS * 2, B_QK_DIM)
        t = jnp.concatenate([_rotate(t[..., :B_ROT_DIM], rot_ang), t[..., B_ROT_DIM:]], axis=-1)
        return t.reshape(b, s, B_HEADS, 2, B_QK_DIM)

    q = partial_rot(q) * (B_QK_DIM ** -0.5)
    k = partial_rot(k)
    v = v.reshape(b, s, B_HEADS, B_V_DIM)
    lf = lam_p.astype(jnp.float32)
    lam = jnp.exp(jnp.sum(lf[0] * lf[1])) - jnp.exp(jnp.sum(lf[2] * lf[3])) + lambda_init

    def block(qb):
        sc = jnp.einsum('bqhcd,bshcd->bhcqs', qb, k)
        p = _softmax_f32(sc)
        p = (p[:, :, 0] - lam * p[:, :, 1]).astype(v.dtype)
        return jnp.einsum('bhqs,bshe->bqhe', p, v)

    o = _sweep_query_blocks(block, q)
    o = _rms_norm(o, sub_g) * (1.0 - lambda_init)
    return o.reshape(b, s, B_HEADS * B_V_DIM)


def _mla(c_q, c_kv, k_r, q_g, kv_g, w_uq, w_ukv, ang):
    b, s = c_q.shape[:2]
    q = jnp.einsum('bsr,rn->bsn', _rms_norm(c_q, q_g), w_uq).reshape(b, s, C_HEADS, C_NOPE_DIM + C_ROPE_DIM)
    kv = jnp.einsum('bsr,rn->bsn', _rms_norm(c_kv, kv_g), w_ukv).reshape(b, s, C_HEADS, C_NOPE_DIM + C_V_DIM)
    scale = (C_NOPE_DIM + C_ROPE_DIM) ** -0.5
    q_nope = q[..., :C_NOPE_DIM] * scale
    q_rope = _rotate(q[..., C_NOPE_DIM:], ang) * scale
    k_nope, v = kv[..., :C_NOPE_DIM], kv[..., C_NOPE_DIM:]
    k_rope = _rotate(k_r[:, :, None, :], ang)[:, :, 0]

    def block(qn, qr):
        sc = jnp.einsum('bqhd,bshd->bhqs', qn, k_nope) + jnp.einsum('bqhr,bsr->bhqs', qr, k_rope)
        p = _softmax_f32(sc).astype(v.dtype)
        return jnp.einsum('bhqs,bshe->bqhe', p, v)

    o = _sweep_query_blocks(block, q_nope, q_rope)
    return o.reshape(b, s, C_HEADS * C_V_DIM)


def setup_inputs(seed: int = 0) -> dict:
    key = jax.random.key(seed)
    ks = jax.random.split(key, 20)
    f32 = jnp.float32

    def nrm(k, shape, scale):
        return jax.random.normal(k, shape, f32) * scale

    def gain(k, shape):
        return 1.0 + 0.02 * jax.random.normal(k, shape, f32)

    return {
        'x': jax.random.normal(ks[0], (BATCH, SEQ, D_MODEL), f32),
        'ln1_g': gain(ks[1], (DEPTH, D_MODEL)),
        'w_in': nrm(ks[2], (DEPTH, D_MODEL, IN_COLS), D_MODEL ** -0.5),
        'a_q_norm': gain(ks[3], (DEPTH, A_HEAD_DIM)),
        'a_k_norm': gain(ks[4], (DEPTH, A_HEAD_DIM)),
        'b_lambda': nrm(ks[5], (DEPTH, 4, B_QK_DIM), 0.1),
        'b_subln': gain(ks[6], (DEPTH, B_V_DIM)),
        'c_q_norm': gain(ks[7], (DEPTH, C_Q_RANK)),
        'c_kv_norm': gain(ks[8], (DEPTH, C_KV_RANK)),
        'c_w_uq': nrm(ks[9], (DEPTH, C_Q_RANK, C_HEADS * (C_NOPE_DIM + C_ROPE_DIM)), C_Q_RANK ** -0.5),
        'c_w_ukv': nrm(ks[10], (DEPTH, C_KV_RANK, C_HEADS * (C_NOPE_DIM + C_V_DIM)), C_KV_RANK ** -0.5),
        'w_branch': nrm(ks[11], (DEPTH, N_BRANCH, BRANCH_W, D_MODEL), BRANCH_W ** -0.5),
        'w_out': nrm(ks[12], (DEPTH, D_MODEL, D_MODEL), D_MODEL ** -0.5),
        'ln2_g': gain(ks[13], (DEPTH, D_MODEL)),
        'w_ff1': nrm(ks[14], (DEPTH, D_MODEL, D_FF), D_MODEL ** -0.5),
        'w_ff2': nrm(ks[15], (DEPTH, D_FF, D_MODEL), D_FF ** -0.5),
        'final_g': gain(ks[16], (D_MODEL,)),
    }


def reference(x, ln1_g, w_in, a_q_norm, a_k_norm, b_lambda, b_subln, c_q_norm, c_kv_norm,
              c_w_uq, c_w_ukv, w_branch, w_out, ln2_g, w_ff1, w_ff2, final_g):
    b, s, d = x.shape
    n_rows = s // GRID_W
    t = jnp.arange(s)
    row_idx = jnp.repeat(jnp.arange(n_rows), GRID_W)
    col_idx = jnp.tile(jnp.arange(GRID_W), n_rows)
    half = A_HEAD_DIM // 2
    row_ang = _angles(row_idx, half, AXIAL_THETA)
    col_ang = _angles(col_idx, half, AXIAL_THETA)
    b_ang = _angles(t, B_ROT_DIM, ROPE_THETA)
    c_ang = _angles(t, C_ROPE_DIM, MLA_THETA)

    for l in range(DEPTH):
        h = _rms_norm(x, ln1_g[l])
        proj = jnp.einsum('bsd,dn->bsn', h, w_in[l])
        (a_q, a_k, a_v, b_q, b_k, b_v, c_cq, c_ckv, c_kr, gate_logits) = _split_columns(proj)

        y_a = _gqa_axial(a_q.reshape(b, s, A_HEADS, A_HEAD_DIM),
                         a_k.reshape(b, s, A_KV_HEADS, A_HEAD_DIM),
                         a_v.reshape(b, s, A_KV_HEADS, A_HEAD_DIM),
                         a_q_norm[l], a_k_norm[l], row_ang, col_ang)
        lambda_init = 0.8 - 0.6 * math.exp(-0.3 * l)
        y_b = _diff_attention(b_q, b_k, b_v, b_lambda[l], b_subln[l], lambda_init, b_ang)
        y_c = _mla(c_cq, c_ckv, c_kr, c_q_norm[l], c_kv_norm[l], c_w_uq[l], c_w_ukv[l], c_ang)

        y = jnp.stack([y_a, y_b, y_c], axis=2)
        gates = jax.nn.sigmoid(gate_logits.reshape(b, s, N_BRANCH, d))
        merged = jnp.sum(gates * jnp.einsum('bsne,ned->bsnd', y, w_branch[l]), axis=2)
        x = x + jnp.einsum('bsd,de->bse', merged, w_out[l])

        h2 = _rms_norm(x, ln2_g[l])
        ff = jnp.square(jax.nn.relu(jnp.einsum('bsd,df->bsf', h2, w_ff1[l])))
        x = x + jnp.einsum('bsf,fd->bsd', ff, w_ff2[l])

    return _rms_norm(x, final_g)
```

```python
import functools
import math

import jax
import jax.numpy as jnp
from jax import lax
from jax.experimental import pallas as pl
from jax.experimental.pallas import tpu as pltpu

F32 = jnp.float32
BF16 = jnp.bfloat16

D_MODEL = 1024
DEPTH = 4
GRID_W = 64
EPS = 1e-6

A_HEADS = 8
A_KV_HEADS = 2
A_HEAD_DIM = 64
AXIAL_THETA = 10000.0

B_HEADS = 8
B_QK_DIM = 32
B_V_DIM = 64
B_ROT_DIM = 8
ROPE_THETA = 500000.0

C_HEADS = 8
C_Q_RANK = 384
C_KV_RANK = 256
C_NOPE_DIM = 64
C_ROPE_DIM = 32
C_V_DIM = 64
MLA_THETA = 10000.0

N_BRANCH = 3
BRANCH_W = 512
D_FF = 4 * D_MODEL

LANES = 128
VMEM_LIMIT = 52 * 1024 * 1024

OFF_AQ, OFF_AK, OFF_AV = 0, 512, 640
OFF_BQ, OFF_BK, OFF_BV = 768, 1280, 1792
OFF_CQ, OFF_CKV, OFF_CKR = 2304, 2688, 2944
PROJ_COLS = 3072

TM_PROJ = 256
TM_MERGE = 256
TM_MLP = 256
TQ = 1024
RC = 256

A_HEAD_ORDER = (0, 4, 1, 5, 2, 6, 3, 7)


def _cparams(n_axes):
    return pltpu.CompilerParams(
        dimension_semantics=("parallel",) * n_axes,
        vmem_limit_bytes=VMEM_LIMIT,
    )


def _const_spec(shape):
    nd = len(shape)
    return pl.BlockSpec(shape, lambda *_: (0,) * nd)


def _rms(x, g):
    return x * lax.rsqrt(jnp.mean(x * x, axis=-1, keepdims=True) + EPS) * g


def _lane_ids(shape):
    return lax.broadcasted_iota(jnp.int32, shape, len(shape) - 1)


def _rope_block(x, cos, sin_signed, first, shift):
    up = pltpu.roll(x, LANES - shift, 1)
    dn = pltpu.roll(x, shift, 1)
    return x * cos + jnp.where(first, up, dn) * sin_signed


def _rope(x, cos, sin_signed, first, shift):
    blocks = [
        _rope_block(x[:, i * LANES:(i + 1) * LANES], cos, sin_signed, first, shift)
        for i in range(x.shape[1] // LANES)
    ]
    return blocks[0] if len(blocks) == 1 else jnp.concatenate(blocks, axis=1)


def _group_sum(x, gmat):
    hi = x.astype(BF16)
    lo = (x - hi.astype(F32)).astype(BF16)
    return (jnp.dot(hi, gmat, preferred_element_type=F32)
            + jnp.dot(lo, gmat, preferred_element_type=F32))


def _with_ones(v):
    ones = jnp.ones((v.shape[0], LANES), v.dtype)
    parts = []
    for i in range(v.shape[1] // LANES):
        parts += [v[:, i * LANES:(i + 1) * LANES], ones]
    return jnp.concatenate(parts, axis=1)


def _proj_kernel(x_ref, g1_ref, w_ref, aqg_ref, akg_ref, cqg_ref, ckvg_ref,
                 wuq_ref, wkn_ref, wv_ref, gmat_ref,
                 cosa_ref, sina_ref, cosb_ref, sinb_ref, cosc_ref, sinc_ref,
                 qa_ref, ka_ref, va_ref, qb_ref, kb_ref, vb_ref,
                 qc_ref, kc_ref, vc_ref):
    x = x_ref[...]
    h = _rms(x, g1_ref[...]).astype(BF16)
    p = jnp.dot(h, w_ref[...], preferred_element_type=F32)

    lane = _lane_ids((1, LANES))
    first_a = (lane % 32) < 16
    first_b = (lane % 32) < 4
    first_c = (lane >= 64) & (lane < 80)
    cosa, sina = cosa_ref[...], sina_ref[...]
    cosb, sinb = cosb_ref[...], sinb_ref[...]
    cosc, sinc = cosc_ref[...], sinc_ref[...]
    gmat = gmat_ref[...]

    def head_norm(t, g, gm):
        ms = _group_sum(t * t, gm) * (1.0 / A_HEAD_DIM)
        return t * lax.rsqrt(ms + EPS) * g

    aq = head_norm(p[:, OFF_AQ:OFF_AQ + 512], aqg_ref[...], gmat)
    qa_ref[...] = (_rope(aq, cosa, sina, first_a, 16)
                   * (A_HEAD_DIM ** -0.5)).astype(BF16)
    ak = head_norm(p[:, OFF_AK:OFF_AK + 128], akg_ref[...], gmat[:LANES, :LANES])
    ka_ref[...] = _rope(ak, cosa, sina, first_a, 16).astype(BF16)
    va_ref[...] = _with_ones(p[:, OFF_AV:OFF_AV + 128].astype(BF16))

    qb_ref[...] = (_rope(p[:, OFF_BQ:OFF_BQ + 512], cosb, sinb, first_b, 4)
                   * (B_QK_DIM ** -0.5)).astype(BF16)
    kb_ref[...] = _rope(p[:, OFF_BK:OFF_BK + 512], cosb, sinb, first_b, 4).astype(BF16)
    vb_ref[...] = _with_ones(p[:, OFF_BV:OFF_BV + 512].astype(BF16))

    cq = _rms(p[:, OFF_CQ:OFF_CQ + C_Q_RANK], cqg_ref[...]).astype(BF16)
    q_up = jnp.dot(cq, wuq_ref[...], preferred_element_type=F32)
    qc_ref[...] = (_rope(q_up, cosc, sinc, first_c, 16)
                   * ((C_NOPE_DIM + C_ROPE_DIM) ** -0.5)).astype(BF16)
    ckv = _rms(p[:, OFF_CKV:OFF_CKV + C_KV_RANK], ckvg_ref[...]).astype(BF16)
    k_nope = jnp.dot(ckv, wkn_ref[...], preferred_element_type=F32)
    k_rope = _rope(p[:, OFF_CKR:OFF_CKR + LANES], cosc, sinc, first_c, 16)
    kc_ref[...] = (k_nope + jnp.concatenate([k_rope] * C_HEADS, axis=1)).astype(BF16)
    v_c = jnp.dot(ckv, wv_ref[...], preferred_element_type=F32)
    vc_ref[...] = _with_ones(v_c.astype(BF16))


def _proj_call(x2, g1, w_main, aqg, akg, cqg, ckvg, wuq, wkn, wv, gmat, tables, seq):
    n = x2.shape[0]
    tm = TM_PROJ
    nt_seq = seq // tm
    tab_spec = pl.BlockSpec((tm, LANES), lambda i: (i % nt_seq, 0))

    def tok_spec(w):
        return pl.BlockSpec((tm, w), lambda i: (i, 0))

    out_widths = (512, 128, 256, 512, 512, 1024, 1024, 1024, 1024)
    return pl.pallas_call(
        _proj_kernel,
        grid=(n // tm,),
        in_specs=[tok_spec(D_MODEL), _const_spec(g1.shape), _const_spec(w_main.shape),
                  _const_spec(aqg.shape), _const_spec(akg.shape),
                  _const_spec(cqg.shape), _const_spec(ckvg.shape),
                  _const_spec(wuq.shape), _const_spec(wkn.shape), _const_spec(wv.shape),
                  _const_spec(gmat.shape)] + [tab_spec] * 6,
        out_specs=[tok_spec(w) for w in out_widths],
        out_shape=[jax.ShapeDtypeStruct((n, w), BF16) for w in out_widths],
        compiler_params=_cparams(1),
        name="proj_prep",
    )(x2, g1, w_main, aqg, akg, cqg, ckvg, wuq, wkn, wv, gmat, *tables)


def _softmax_pv(q, k, v):
    s = lax.dot_general(q, k, (((1,), (1,)), ((), ())), preferred_element_type=F32)
    m = jnp.max(s, axis=-1, keepdims=True)
    e = jnp.exp(s - m).astype(BF16)
    r = jnp.dot(e, v, preferred_element_type=F32)
    return r[:, :LANES], r[:, LANES:]


def _row_loop(tq, body):
    def step(c, carry):
        body(pl.ds(pl.multiple_of(c * RC, RC), RC))
        return carry
    lax.fori_loop(0, tq // RC, step, 0)


def _attn_a_kernel(q_ref, k_ref, v_ref, o_ref):
    lane = _lane_ids((1, LANES))
    low = lane < 64

    def chunk(rows):
        q = q_ref[0, rows, :]
        k = k_ref[0]
        v = v_ref[0]
        zero = jnp.zeros_like(q)
        n0, l0 = _softmax_pv(jnp.where(low, q, zero), k, v)
        n1, l1 = _softmax_pv(jnp.where(low, zero, q), k, v)
        o_ref[0, rows, :] = jnp.where(low, n0 / l0, n1 / l1).astype(o_ref.dtype)

    _row_loop(q_ref.shape[1], chunk)


def _attn_b_kernel(q_ref, k_ref, v_ref, lam_ref, linit_ref, subg_ref, o_ref):
    lane = _lane_ids((1, LANES))
    lf = lam_ref[...]
    lam = (jnp.exp(jnp.sum(lf[0:1] * lf[1:2], axis=-1, keepdims=True))
           - jnp.exp(jnp.sum(lf[2:3] * lf[3:4], axis=-1, keepdims=True))
           + linit_ref[:, 0:1])
    post = subg_ref[...] * (1.0 - linit_ref[...])
    low = lane < 64

    def chunk(rows):
        q = q_ref[0, rows, :]
        k = k_ref[0]
        v = v_ref[0]
        zero = jnp.zeros_like(q)
        heads = []
        for hh in range(2):
            comps = []
            for c in range(2):
                lo_lane = hh * 64 + c * B_QK_DIM
                sel = (lane >= lo_lane) & (lane < lo_lane + B_QK_DIM)
                n, l = _softmax_pv(jnp.where(sel, q, zero), k, v)
                comps.append(n / l)
            o = comps[0] - lam * comps[1]
            in_head = low if hh == 0 else jnp.logical_not(low)
            ms = jnp.sum(jnp.where(in_head, o * o, 0.0), axis=-1,
                         keepdims=True) * (1.0 / B_V_DIM)
            heads.append(o * lax.rsqrt(ms + EPS))
        o_ref[0, rows, :] = (jnp.where(low, heads[0], heads[1]) * post
                             ).astype(o_ref.dtype)

    _row_loop(q_ref.shape[1], chunk)


def _attn_c_kernel(q_ref, k_ref, v_ref, o_ref):
    lane = _lane_ids((1, LANES))
    low = lane < 64

    def chunk(rows):
        v = v_ref[0]
        outs = []
        for hh in range(2):
            q = q_ref[0, rows, hh * LANES:(hh + 1) * LANES]
            k = k_ref[0, :, hh * LANES:(hh + 1) * LANES]
            n, l = _softmax_pv(q, k, v)
            outs.append(n / l)
        o_ref[0, rows, :] = jnp.where(low, outs[0], outs[1]).astype(o_ref.dtype)

    _row_loop(q_ref.shape[1], chunk)


def _attn_call(kernel, name, q, k, v, extra, wq, shared_kv):
    b, s, _ = q.shape
    tq = min(TQ, s)
    groups = BRANCH_W // LANES
    kv_idx = (lambda bi, g, i: (bi, 0, 0)) if shared_kv else (lambda bi, g, i: (bi, 0, g))
    in_specs = [
        pl.BlockSpec((1, tq, wq), lambda bi, g, i: (bi, i, g)),
        pl.BlockSpec((1, s, wq), kv_idx),
        pl.BlockSpec((1, s, 2 * LANES), kv_idx),
    ] + [_const_spec(e.shape) for e in extra]
    return pl.pallas_call(
        kernel,
        grid=(b, groups, s // tq),
        in_specs=in_specs,
        out_specs=pl.BlockSpec((1, tq, LANES), lambda bi, g, i: (bi, i, g)),
        out_shape=jax.ShapeDtypeStruct((b, s, BRANCH_W), BF16),
        compiler_params=_cparams(3),
        name=name,
    )(q, k, v, *extra)


def _merge_kernel(x_ref, g1_ref, ya_ref, yb_ref, yc_ref, wg_ref, wb_ref, wo_ref, o_ref):
    x = x_ref[...]
    h = _rms(x, g1_ref[...]).astype(BF16)
    merged = None
    for n, y_ref in enumerate((ya_ref, yb_ref, yc_ref)):
        logits = jnp.dot(h, wg_ref[:, n * D_MODEL:(n + 1) * D_MODEL],
                         preferred_element_type=F32)
        z = jnp.dot(y_ref[...], wb_ref[n], preferred_element_type=F32)
        term = jax.nn.sigmoid(logits) * z
        merged = term if merged is None else merged + term
    o_ref[...] = x + jnp.dot(merged.astype(BF16), wo_ref[...],
                             preferred_element_type=F32)


def _merge_call(x2, g1, ya, yb, yc, wg, wb, wo):
    n = x2.shape[0]
    tm = TM_MERGE

    def tok_spec(w):
        return pl.BlockSpec((tm, w), lambda i: (i, 0))

    return pl.pallas_call(
        _merge_kernel,
        grid=(n // tm,),
        in_specs=[tok_spec(D_MODEL), _const_spec(g1.shape),
                  tok_spec(BRANCH_W), tok_spec(BRANCH_W), tok_spec(BRANCH_W),
                  _const_spec(wg.shape), _const_spec(wb.shape), _const_spec(wo.shape)],
        out_specs=tok_spec(D_MODEL),
        out_shape=jax.ShapeDtypeStruct((n, D_MODEL), F32),
        compiler_params=_cparams(1),
        name="gated_merge",
    )(x2, g1, ya, yb, yc, wg, wb, wo)


FF_CHUNK = 1024


def _mlp_kernel(x_ref, g2_ref, w1_ref, w2_ref, gf_ref, o_ref, *, final):
    x = x_ref[...]
    h = _rms(x, g2_ref[...]).astype(BF16)
    acc = x
    for c in range(D_FF // FF_CHUNK):
        cols = pl.ds(c * FF_CHUNK, FF_CHUNK)
        hid = jnp.dot(h, w1_ref[:, cols], preferred_element_type=F32)
        hid = jnp.square(jnp.maximum(hid, 0.0)).astype(BF16)
        acc = acc + jnp.dot(hid, w2_ref[cols, :], preferred_element_type=F32)
    if final:
        acc = _rms(acc, gf_ref[...])
    o_ref[...] = acc


def _mlp_call(x2, g2, w1, w2, gf, final):
    n = x2.shape[0]
    tm = TM_MLP
    tok = pl.BlockSpec((tm, D_MODEL), lambda i: (i, 0))
    return pl.pallas_call(
        functools.partial(_mlp_kernel, final=final),
        grid=(n // tm,),
        in_specs=[tok, _const_spec(g2.shape), _const_spec(w1.shape),
                  _const_spec(w2.shape), _const_spec(gf.shape)],
        out_specs=tok,
        out_shape=jax.ShapeDtypeStruct((n, D_MODEL), F32),
        compiler_params=_cparams(1),
        name="mlp_final" if final else "mlp",
    )(x2, g2, w1, w2, gf)


def _angles(pos, dim, theta):
    inv_freq = theta ** (-jnp.arange(0, dim, 2, dtype=F32) / dim)
    return pos.astype(F32)[:, None] * inv_freq[None, :]


def _rope_tables(seq):
    t = jnp.arange(seq)
    half = A_HEAD_DIM // 2
    row_ang = _angles(t // GRID_W, half, AXIAL_THETA)
    col_ang = _angles(t % GRID_W, half, AXIAL_THETA)
    b_ang = _angles(t, B_ROT_DIM, ROPE_THETA)
    c_ang = _angles(t, C_ROPE_DIM, MLA_THETA)

    def pair(ang):
        c, s = jnp.cos(ang), jnp.sin(ang)
        return jnp.concatenate([c, c], 1), jnp.concatenate([-s, s], 1)

    one = lambda w: jnp.ones((seq, w), F32)
    zero = lambda w: jnp.zeros((seq, w), F32)

    rc, rs = pair(row_ang)
    cc, cs = pair(col_ang)
    cos_a = jnp.concatenate([rc, cc] * 2, 1)
    sin_a = jnp.concatenate([rs, cs] * 2, 1)

    bc, bs = pair(b_ang)
    cos_b = jnp.concatenate([bc, one(B_QK_DIM - B_ROT_DIM)] * 4, 1)
    sin_b = jnp.concatenate([bs, zero(B_QK_DIM - B_ROT_DIM)] * 4, 1)

    mc, ms = pair(c_ang)
    cos_c = jnp.concatenate([one(64), mc, one(32)], 1)
    sin_c = jnp.concatenate([zero(64), ms, zero(32)], 1)
    return cos_a, sin_a, cos_b, sin_b, cos_c, sin_c


def _layer_weights(l, w_in, c_w_uq, c_w_ukv, w_branch):
    w = w_in[l]
    a_q = w[:, 0:512].reshape(D_MODEL, A_HEADS, A_HEAD_DIM)
    a_q = a_q[:, jnp.array(A_HEAD_ORDER), :].reshape(D_MODEL, 512)
    c_kr = jnp.zeros((D_MODEL, LANES), F32).at[:, 64:96].set(w[:, 2944:2976])
    w_main = jnp.concatenate([a_q, w[:, 512:2944], c_kr], axis=1).astype(BF16)
    w_gate = w[:, 2976:].astype(BF16)

    uq = c_w_uq[l].reshape(C_Q_RANK, C_HEADS, C_NOPE_DIM + C_ROPE_DIM)
    uq = jnp.pad(uq, ((0, 0), (0, 0), (0, LANES - C_NOPE_DIM - C_ROPE_DIM)))
    wuq = uq.reshape(C_Q_RANK, C_HEADS * LANES).astype(BF16)
    ukv = c_w_ukv[l].reshape(C_KV_RANK, C_HEADS, C_NOPE_DIM + C_V_DIM)
    kn = jnp.pad(ukv[:, :, :C_NOPE_DIM], ((0, 0), (0, 0), (0, LANES - C_NOPE_DIM)))
    wkn = kn.reshape(C_KV_RANK, C_HEADS * LANES).astype(BF16)
    wv = ukv[:, :, C_NOPE_DIM:].reshape(C_KV_RANK, C_HEADS * C_V_DIM).astype(BF16)

    wb = w_branch[l]
    wb_a = wb[0].reshape(A_HEADS, A_HEAD_DIM, D_MODEL)[jnp.array(A_HEAD_ORDER)]
    wb = jnp.stack([wb_a.reshape(BRANCH_W, D_MODEL), wb[1], wb[2]]).astype(BF16)
    return w_main, w_gate, wuq, wkn, wv, wb


def kernel(x, ln1_g, w_in, a_q_norm, a_k_norm, b_lambda, b_subln, c_q_norm, c_kv_norm,
           c_w_uq, c_w_ukv, w_branch, w_out, ln2_g, w_ff1, w_ff2, final_g):
    b, s, d = x.shape
    n = b * s
    tables = _rope_tables(s)
    gidx = jnp.arange(BRANCH_W) // A_HEAD_DIM
    gmat = (gidx[:, None] == gidx[None, :]).astype(BF16)
    gf = final_g.reshape(1, d)

    x2 = x.reshape(n, d)
    for l in range(DEPTH):
        w_main, w_gate, wuq, wkn, wv, wb = _layer_weights(l, w_in, c_w_uq, c_w_ukv, w_branch)
        g1 = ln1_g[l].reshape(1, d)
        qa, ka, va, qb, kb, vb, qc, kc, vc = _proj_call(
            x2, g1, w_main,
            jnp.tile(a_q_norm[l], A_HEADS).reshape(1, 512),
            jnp.tile(a_k_norm[l], A_KV_HEADS).reshape(1, 128),
            c_q_norm[l].reshape(1, C_Q_RANK), c_kv_norm[l].reshape(1, C_KV_RANK),
            wuq, wkn, wv, gmat, tables, s)

        r3 = lambda t: t.reshape(b, s, t.shape[-1])
        ya = _attn_call(_attn_a_kernel, "attn_gqa", r3(qa), r3(ka), r3(va), (),
                        LANES, True)
        lambda_init = 0.8 - 0.6 * math.exp(-0.3 * l)
        yb = _attn_call(_attn_b_kernel, "attn_diff", r3(qb), r3(kb), r3(vb),
                        (b_lambda[l], jnp.full((1, LANES), lambda_init, F32),
                         jnp.tile(b_subln[l], 2).reshape(1, LANES)),
                        LANES, False)
        yc = _attn_call(_attn_c_kernel, "attn_mla", r3(qc), r3(kc), r3(vc), (),
                        2 * LANES, False)

        x2 = _merge_call(x2, g1, ya.reshape(n, BRANCH_W), yb.reshape(n, BRANCH_W),
                         yc.reshape(n, BRANCH_W), w_gate, wb, w_out[l].astype(BF16))
        x2 = _mlp_call(x2, ln2_g[l].reshape(1, d), w_ff1[l].astype(BF16),
                       w_ff2[l].astype(BF16), gf, l == DEPTH - 1)
    return x2.reshape(b, s, d)
```

```python
import functools
import math

import jax
import jax.numpy as jnp
from jax import lax
from jax.experimental import pallas as pl
from jax.experimental.pallas import tpu as pltpu

F32 = jnp.float32
BF16 = jnp.bfloat16

D_MODEL = 1024
DEPTH = 4
GRID_W = 64
EPS = 1e-6

A_HEADS = 8
A_KV_HEADS = 2
A_HEAD_DIM = 64
AXIAL_THETA = 10000.0

B_HEADS = 8
B_QK_DIM = 32
B_V_DIM = 64
B_ROT_DIM = 8
ROPE_THETA = 500000.0

C_HEADS = 8
C_Q_RANK = 384
C_KV_RANK = 256
C_NOPE_DIM = 64
C_ROPE_DIM = 32
C_V_DIM = 64
MLA_THETA = 10000.0

N_BRANCH = 3
BRANCH_W = 512
D_FF = 4 * D_MODEL

LANES = 128
VMEM_LIMIT = 52 * 1024 * 1024
OUT_GROUPS = BRANCH_W // LANES

OFF_AQ, OFF_AK, OFF_AV = 0, 512, 640
OFF_BQ, OFF_BK, OFF_BV = 768, 1280, 1792
OFF_CQ, OFF_CKV, OFF_CKR = 2304, 2688, 2944
PROJ_COLS = 3072

TM_PROJ = 256
TM_MERGE = 256
TM_MLP = 256

A_HEAD_ORDER = (0, 4, 1, 5, 2, 6, 3, 7)


def _cparams(n_axes):
    return pltpu.CompilerParams(
        dimension_semantics=("parallel",) * n_axes,
        vmem_limit_bytes=VMEM_LIMIT,
    )


def _const_spec(shape):
    nd = len(shape)
    return pl.BlockSpec(shape, lambda *_: (0,) * nd)


def _rms(x, g):
    return x * lax.rsqrt(jnp.mean(x * x, axis=-1, keepdims=True) + EPS) * g


def _lane_ids(shape):
    return lax.broadcasted_iota(jnp.int32, shape, len(shape) - 1)


def _rope_block(x, cos, sin_signed, first, shift):
    up = pltpu.roll(x, LANES - shift, 1)
    dn = pltpu.roll(x, shift, 1)
    return x * cos + jnp.where(first, up, dn) * sin_signed


def _rope(x, cos, sin_signed, first, shift):
    blocks = [
        _rope_block(x[:, i * LANES:(i + 1) * LANES], cos, sin_signed, first, shift)
        for i in range(x.shape[1] // LANES)
    ]
    return blocks[0] if len(blocks) == 1 else jnp.concatenate(blocks, axis=1)


def _group_sum(x, gmat):
    hi = x.astype(BF16)
    lo = (x - hi.astype(F32)).astype(BF16)
    return (jnp.dot(hi, gmat, preferred_element_type=F32)
            + jnp.dot(lo, gmat, preferred_element_type=F32))


def _store_groups(ref, x):
    for g in range(x.shape[1] // LANES):
        ref[0, g] = x[:, g * LANES:(g + 1) * LANES].astype(ref.dtype)


def _store_groups_t(ref, x):
    for g in range(x.shape[1] // LANES):
        ref[0, g] = x[:, g * LANES:(g + 1) * LANES].T.astype(ref.dtype)


def _store_values(ref, v):
    ones = jnp.ones((v.shape[0], LANES), ref.dtype)
    for g in range(v.shape[1] // LANES):
        ref[0, g] = jnp.concatenate(
            [v[:, g * LANES:(g + 1) * LANES].astype(ref.dtype), ones], axis=1)


def _proj_kernel(x_ref, g1_ref, w_ref, aqg_ref, akg_ref, cqg_ref, ckvg_ref,
                 wuq_ref, wkn_ref, wv_ref, gmat_ref,
                 cosa_ref, sina_ref, cosb_ref, sinb_ref, cosc_ref, sinc_ref,
                 qa_ref, ka_ref, va_ref, qb_ref, kb_ref, vb_ref,
                 qc_ref, kc_ref, vc_ref):
    x = x_ref[...]
    h = _rms(x, g1_ref[...]).astype(BF16)
    p = jnp.dot(h, w_ref[...], preferred_element_type=F32)

    lane = _lane_ids((1, LANES))
    first_a = (lane % 32) < 16
    first_b = (lane % 32) < 4
    first_c = (lane >= 64) & (lane < 80)
    cosa, sina = cosa_ref[...], sina_ref[...]
    cosb, sinb = cosb_ref[...], sinb_ref[...]
    cosc, sinc = cosc_ref[...], sinc_ref[...]
    gmat = gmat_ref[...]

    def head_norm(t, g, gm):
        ms = _group_sum(t * t, gm) * (1.0 / A_HEAD_DIM)
        return t * lax.rsqrt(ms + EPS) * g

    aq = head_norm(p[:, OFF_AQ:OFF_AQ + 512], aqg_ref[...], gmat)
    _store_groups(qa_ref, _rope(aq, cosa, sina, first_a, 16) * (A_HEAD_DIM ** -0.5))
    ak = head_norm(p[:, OFF_AK:OFF_AK + 128], akg_ref[...], gmat[:LANES, :LANES])
    _store_groups_t(ka_ref, _rope(ak, cosa, sina, first_a, 16))
    _store_values(va_ref, p[:, OFF_AV:OFF_AV + 128])

    _store_groups(qb_ref, _rope(p[:, OFF_BQ:OFF_BQ + 512], cosb, sinb, first_b, 4)
                  * (B_QK_DIM ** -0.5))
    _store_groups_t(kb_ref, _rope(p[:, OFF_BK:OFF_BK + 512], cosb, sinb, first_b, 4))
    _store_values(vb_ref, p[:, OFF_BV:OFF_BV + 512])

    cq = _rms(p[:, OFF_CQ:OFF_CQ + C_Q_RANK], cqg_ref[...]).astype(BF16)
    q_up = jnp.dot(cq, wuq_ref[...], preferred_element_type=F32)
    _store_groups(qc_ref, _rope(q_up, cosc, sinc, first_c, 16)
                  * ((C_NOPE_DIM + C_ROPE_DIM) ** -0.5))
    ckv = _rms(p[:, OFF_CKV:OFF_CKV + C_KV_RANK], ckvg_ref[...]).astype(BF16)
    k_nope = jnp.dot(ckv, wkn_ref[...], preferred_element_type=F32)
    k_rope = _rope(p[:, OFF_CKR:OFF_CKR + LANES], cosc, sinc, first_c, 16)
    _store_groups_t(kc_ref, k_nope + jnp.concatenate([k_rope] * C_HEADS, axis=1))
    _store_values(vc_ref, jnp.dot(ckv, wv_ref[...], preferred_element_type=F32))


def _proj_call(x2, g1, w_main, aqg, akg, cqg, ckvg, wuq, wkn, wv, gmat, tables, batch, seq):
    n = x2.shape[0]
    tm = TM_PROJ
    nt = seq // tm
    tab_spec = pl.BlockSpec((tm, LANES), lambda i: (i % nt, 0))

    def rows(g, w):
        return (jax.ShapeDtypeStruct((batch, g, seq, w), BF16),
                pl.BlockSpec((1, g, tm, w), lambda i: (i // nt, 0, i % nt, 0)))

    def cols(g):
        return (jax.ShapeDtypeStruct((batch, g, LANES, seq), BF16),
                pl.BlockSpec((1, g, LANES, tm), lambda i: (i // nt, 0, 0, i % nt)))

    outs = [rows(4, LANES), cols(1), rows(1, 2 * LANES),
            rows(4, LANES), cols(4), rows(4, 2 * LANES),
            rows(8, LANES), cols(8), rows(4, 2 * LANES)]
    return pl.pallas_call(
        _proj_kernel,
        grid=(n // tm,),
        in_specs=[pl.BlockSpec((tm, D_MODEL), lambda i: (i, 0)),
                  _const_spec(g1.shape), _const_spec(w_main.shape),
                  _const_spec(aqg.shape), _const_spec(akg.shape),
                  _const_spec(cqg.shape), _const_spec(ckvg.shape),
                  _const_spec(wuq.shape), _const_spec(wkn.shape), _const_spec(wv.shape),
                  _const_spec(gmat.shape)] + [tab_spec] * 6,
        out_specs=[o[1] for o in outs],
        out_shape=[o[0] for o in outs],
        compiler_params=_cparams(1),
        name="proj_prep",
    )(x2, g1, w_main, aqg, akg, cqg, ckvg, wuq, wkn, wv, gmat, *tables)


def _attn_pipeline(n_items, rows_per_item, n_sub, get_qk, get_v, finalize, o_ref,
                   s_buf, m_buf, e_buf):
    n_chunks = o_ref.shape[2] // rows_per_item
    assert n_chunks & (n_chunks - 1) == 0 and n_items % 2 == 0
    shift = n_chunks.bit_length() - 1

    def locate(t):
        g = t >> shift
        c = t & (n_chunks - 1)
        return g, pl.ds(pl.multiple_of(c * rows_per_item, rows_per_item), rows_per_item)

    def stage1(t, slot):
        g, rows = locate(t)
        for j in range(n_sub):
            q, kt = get_qk(g, rows, j)
            s = jnp.dot(q, kt, preferred_element_type=F32)
            s_buf[slot, j] = s
            m_buf[slot, j] = jnp.max(s, axis=-1, keepdims=True)

    def stage2(slot):
        for j in range(n_sub):
            e_buf[slot, j] = jnp.exp(s_buf[slot, j] - m_buf[slot, j]).astype(BF16)

    def stage3(t, slot):
        g, rows = locate(t)
        outs = []
        for j in range(n_sub):
            r = jnp.dot(e_buf[slot, j], get_v(g, j), preferred_element_type=F32)
            outs.append(r[:, :LANES] / r[:, LANES:])
        o_ref[0, g, rows, :] = finalize(outs).astype(o_ref.dtype)

    i32 = lambda v: jnp.int32(v)
    stage1(i32(0), 0)
    stage1(i32(1), 1)
    stage2(0)

    def body(u, carry):
        t = 2 * u + 1
        stage1(t + 1, 0)
        stage2(1)
        stage3(t - 1, 0)
        stage1(t + 2, 1)
        stage2(0)
        stage3(t, 1)
        return carry

    lax.fori_loop(0, (n_items - 2) // 2, body, 0)
    stage2(1)
    stage3(i32(n_items - 2), 0)
    stage3(i32(n_items - 1), 1)


def _attn_a_kernel(q_ref, k_ref, v_ref, o_ref, s_buf, m_buf, e_buf, *, rc):
    lane = _lane_ids((1, LANES))
    low = lane < 64

    def get_qk(g, rows, j):
        q = q_ref[0, g, rows, :]
        zero = jnp.zeros_like(q)
        return (jnp.where(low, q, zero) if j == 0 else jnp.where(low, zero, q)), k_ref[0, 0]

    def get_v(g, j):
        return v_ref[0, 0]

    def finalize(outs):
        return jnp.where(low, outs[0], outs[1])

    n_items = OUT_GROUPS * (o_ref.shape[2] // rc)
    _attn_pipeline(n_items, rc, 2, get_qk, get_v, finalize, o_ref, s_buf, m_buf, e_buf)


def _attn_b_kernel(q_ref, k_ref, v_ref, lam_ref, linit_ref, subg_ref, o_ref,
                   s_buf, m_buf, e_buf, *, rc):
    lane = _lane_ids((1, LANES))
    lf = lam_ref[...]
    lam = (jnp.exp(jnp.sum(lf[0:1] * lf[1:2], axis=-1, keepdims=True))
           - jnp.exp(jnp.sum(lf[2:3] * lf[3:4], axis=-1, keepdims=True))
           + linit_ref[:, 0:1])
    post = subg_ref[...] * (1.0 - linit_ref[...])
    low = lane < 64

    def get_qk(g, rows, j):
        q = q_ref[0, g, rows, :]
        sel = (lane >= j * B_QK_DIM) & (lane < (j + 1) * B_QK_DIM)
        return jnp.where(sel, q, jnp.zeros_like(q)), k_ref[0, g]

    def get_v(g, j):
        return v_ref[0, g]

    def finalize(outs):
        heads = []
        for hh in range(2):
            o = outs[2 * hh] - lam * outs[2 * hh + 1]
            in_head = low if hh == 0 else jnp.logical_not(low)
            ms = jnp.sum(jnp.where(in_head, o * o, 0.0), axis=-1,
                         keepdims=True) * (1.0 / B_V_DIM)
            heads.append(o * lax.rsqrt(ms + EPS))
        return jnp.where(low, heads[0], heads[1]) * post

    n_items = OUT_GROUPS * (o_ref.shape[2] // rc)
    _attn_pipeline(n_items, rc, 4, get_qk, get_v, finalize, o_ref, s_buf, m_buf, e_buf)


def _attn_c_kernel(q_ref, k_ref, v_ref, o_ref, s_buf, m_buf, e_buf, *, rc):
    lane = _lane_ids((1, LANES))
    low = lane < 64

    def get_qk(g, rows, j):
        return q_ref[0, 2 * g + j, rows, :], k_ref[0, 2 * g + j]

    def get_v(g, j):
        return v_ref[0, g]

    def finalize(outs):
        return jnp.where(low, outs[0], outs[1])

    n_items = OUT_GROUPS * (o_ref.shape[2] // rc)
    _attn_pipeline(n_items, rc, 2, get_qk, get_v, finalize, o_ref, s_buf, m_buf, e_buf)


def _attn_call(kernel, name, q, kt, v, extra, n_sub, rc):
    b, _, s, _ = q.shape
    rc = min(rc, s)

    def batch_spec(a):
        return pl.BlockSpec((1,) + a.shape[1:], lambda bi: (bi, 0, 0, 0))

    return pl.pallas_call(
        functools.partial(kernel, rc=rc),
        grid=(b,),
        in_specs=[batch_spec(q), batch_spec(kt), batch_spec(v)]
        + [_const_spec(e.shape) for e in extra],
        out_specs=pl.BlockSpec((1, OUT_GROUPS, s, LANES), lambda bi: (bi, 0, 0, 0)),
        out_shape=jax.ShapeDtypeStruct((b, OUT_GROUPS, s, LANES), BF16),
        scratch_shapes=[pltpu.VMEM((2, n_sub, rc, s), F32),
                        pltpu.VMEM((2, n_sub, rc, 1), F32),
                        pltpu.VMEM((2, n_sub, rc, s), BF16)],
        compiler_params=_cparams(1),
        name=name,
    )(q, kt, v, *extra)


def _merge_kernel(x_ref, g1_ref, ya_ref, yb_ref, yc_ref, wg_ref, wb_ref, wo_ref, o_ref):
    x = x_ref[...]
    h = _rms(x, g1_ref[...]).astype(BF16)
    merged = None
    for n, y_ref in enumerate((ya_ref, yb_ref, yc_ref)):
        logits = jnp.dot(h, wg_ref[:, n * D_MODEL:(n + 1) * D_MODEL],
                         preferred_element_type=F32)
        y = jnp.concatenate([y_ref[0, g] for g in range(OUT_GROUPS)], axis=1)
        z = jnp.dot(y, wb_ref[n], preferred_element_type=F32)
        term = jax.nn.sigmoid(logits) * z
        merged = term if merged is None else merged + term
    o_ref[...] = x + jnp.dot(merged.astype(BF16), wo_ref[...],
                             preferred_element_type=F32)


def _merge_call(x2, g1, ya, yb, yc, wg, wb, wo):
    n = x2.shape[0]
    tm = TM_MERGE
    nt = ya.shape[2] // tm
    tok = pl.BlockSpec((tm, D_MODEL), lambda i: (i, 0))
    y_spec = pl.BlockSpec((1, OUT_GROUPS, tm, LANES), lambda i: (i // nt, 0, i % nt, 0))
    return pl.pallas_call(
        _merge_kernel,
        grid=(n // tm,),
        in_specs=[tok, _const_spec(g1.shape), y_spec, y_spec, y_spec,
                  _const_spec(wg.shape), _const_spec(wb.shape), _const_spec(wo.shape)],
        out_specs=tok,
        out_shape=jax.ShapeDtypeStruct((n, D_MODEL), F32),
        compiler_params=_cparams(1),
        name="gated_merge",
    )(x2, g1, ya, yb, yc, wg, wb, wo)


FF_CHUNK = 1024


def _mlp_kernel(x_ref, g2_ref, w1_ref, w2_ref, gf_ref, o_ref, *, final):
    x = x_ref[...]
    h = _rms(x, g2_ref[...]).astype(BF16)
    acc = x
    for c in range(D_FF // FF_CHUNK):
        cols = pl.ds(c * FF_CHUNK, FF_CHUNK)
        hid = jnp.dot(h, w1_ref[:, cols], preferred_element_type=F32)
        hid = jnp.square(jnp.maximum(hid, 0.0)).astype(BF16)
        acc = acc + jnp.dot(hid, w2_ref[cols, :], preferred_element_type=F32)
    if final:
        acc = _rms(acc, gf_ref[...])
    o_ref[...] = acc


def _mlp_call(x2, g2, w1, w2, gf, final):
    n = x2.shape[0]
    tm = TM_MLP
    tok = pl.BlockSpec((tm, D_MODEL), lambda i: (i, 0))
    return pl.pallas_call(
        functools.partial(_mlp_kernel, final=final),
        grid=(n // tm,),
        in_specs=[tok, _const_spec(g2.shape), _const_spec(w1.shape),
                  _const_spec(w2.shape), _const_spec(gf.shape)],
        out_specs=tok,
        out_shape=jax.ShapeDtypeStruct((n, D_MODEL), F32),
        compiler_params=_cparams(1),
        name="mlp_final" if final else "mlp",
    )(x2, g2, w1, w2, gf)


def _angles(pos, dim, theta):
    inv_freq = theta ** (-jnp.arange(0, dim, 2, dtype=F32) / dim)
    return pos.astype(F32)[:, None] * inv_freq[None, :]


def _rope_tables(seq):
    t = jnp.arange(seq)
    half = A_HEAD_DIM // 2
    row_ang = _angles(t // GRID_W, half, AXIAL_THETA)
    col_ang = _angles(t % GRID_W, half, AXIAL_THETA)
    b_ang = _angles(t, B_ROT_DIM, ROPE_THETA)
    c_ang = _angles(t, C_ROPE_DIM, MLA_THETA)

    def pair(ang):
        c, s = jnp.cos(ang), jnp.sin(ang)
        return jnp.concatenate([c, c], 1), jnp.concatenate([-s, s], 1)

    one = lambda w: jnp.ones((seq, w), F32)
    zero = lambda w: jnp.zeros((seq, w), F32)

    rc, rs = pair(row_ang)
    cc, cs = pair(col_ang)
    cos_a = jnp.concatenate([rc, cc] * 2, 1)
    sin_a = jnp.concatenate([rs, cs] * 2, 1)

    bc, bs = pair(b_ang)
    cos_b = jnp.concatenate([bc, one(B_QK_DIM - B_ROT_DIM)] * 4, 1)
    sin_b = jnp.concatenate([bs, zero(B_QK_DIM - B_ROT_DIM)] * 4, 1)

    mc, ms = pair(c_ang)
    cos_c = jnp.concatenate([one(64), mc, one(32)], 1)
    sin_c = jnp.concatenate([zero(64), ms, zero(32)], 1)
    return cos_a, sin_a, cos_b, sin_b, cos_c, sin_c


def _layer_weights(l, w_in, c_w_uq, c_w_ukv, w_branch):
    w = w_in[l]
    a_q = w[:, 0:512].reshape(D_MODEL, A_HEADS, A_HEAD_DIM)
    a_q = a_q[:, jnp.array(A_HEAD_ORDER), :].reshape(D_MODEL, 512)
    c_kr = jnp.zeros((D_MODEL, LANES), F32).at[:, 64:96].set(w[:, 2944:2976])
    w_main = jnp.concatenate([a_q, w[:, 512:2944], c_kr], axis=1).astype(BF16)
    w_gate = w[:, 2976:].astype(BF16)

    uq = c_w_uq[l].reshape(C_Q_RANK, C_HEADS, C_NOPE_DIM + C_ROPE_DIM)
    uq = jnp.pad(uq, ((0, 0), (0, 0), (0, LANES - C_NOPE_DIM - C_ROPE_DIM)))
    wuq = uq.reshape(C_Q_RANK, C_HEADS * LANES).astype(BF16)
    ukv = c_w_ukv[l].reshape(C_KV_RANK, C_HEADS, C_NOPE_DIM + C_V_DIM)
    kn = jnp.pad(ukv[:, :, :C_NOPE_DIM], ((0, 0), (0, 0), (0, LANES - C_NOPE_DIM)))
    wkn = kn.reshape(C_KV_RANK, C_HEADS * LANES).astype(BF16)
    wv = ukv[:, :, C_NOPE_DIM:].reshape(C_KV_RANK, C_HEADS * C_V_DIM).astype(BF16)

    wb = w_branch[l]
    wb_a = wb[0].reshape(A_HEADS, A_HEAD_DIM, D_MODEL)[jnp.array(A_HEAD_ORDER)]
    wb = jnp.stack([wb_a.reshape(BRANCH_W, D_MODEL), wb[1], wb[2]]).astype(BF16)
    return w_main, w_gate, wuq, wkn, wv, wb


def kernel(x, ln1_g, w_in, a_q_norm, a_k_norm, b_lambda, b_subln, c_q_norm, c_kv_norm,
           c_w_uq, c_w_ukv, w_branch, w_out, ln2_g, w_ff1, w_ff2, final_g):
    b, s, d = x.shape
    n = b * s
    tables = _rope_tables(s)
    gidx = jnp.arange(BRANCH_W) // A_HEAD_DIM
    gmat = (gidx[:, None] == gidx[None, :]).astype(BF16)
    gf = final_g.reshape(1, d)

    x2 = x.reshape(n, d)
    for l in range(DEPTH):
        w_main, w_gate, wuq, wkn, wv, wb = _layer_weights(l, w_in, c_w_uq, c_w_ukv, w_branch)
        g1 = ln1_g[l].reshape(1, d)
        qa, ka, va, qb, kb, vb, qc, kc, vc = _proj_call(
            x2, g1, w_main,
            jnp.tile(a_q_norm[l], A_HEADS).reshape(1, 512),
            jnp.tile(a_k_norm[l], A_KV_HEADS).reshape(1, 128),
            c_q_norm[l].reshape(1, C_Q_RANK), c_kv_norm[l].reshape(1, C_KV_RANK),
            wuq, wkn, wv, gmat, tables, b, s)

        ya = _attn_call(_attn_a_kernel, "attn_gqa", qa, ka, va, (), 2, 256)
        lambda_init = 0.8 - 0.6 * math.exp(-0.3 * l)
        yb = _attn_call(_attn_b_kernel, "attn_diff", qb, kb, vb,
                        (b_lambda[l], jnp.full((1, LANES), lambda_init, F32),
                         jnp.tile(b_subln[l], 2).reshape(1, LANES)), 4, 128)
        yc = _attn_call(_attn_c_kernel, "attn_mla", qc, kc, vc, (), 2, 256)

        x2 = _merge_call(x2, g1, ya, yb, yc, w_gate, wb, w_out[l].astype(BF16))
        x2 = _mlp_call(x2, ln2_g[l].reshape(1, d), w_ff1[l].astype(BF16),
                       w_ff2[l].astype(BF16), gf, l == DEPTH - 1)
    return x2.reshape(b, s, d)
```

```python
import functools
import math

import jax
import jax.numpy as jnp
from jax import lax
from jax.experimental import pallas as pl
from jax.experimental.pallas import tpu as pltpu

F32 = jnp.float32
BF16 = jnp.bfloat16

D_MODEL = 1024
DEPTH = 4
GRID_W = 64
EPS = 1e-6

A_HEADS = 8
A_KV_HEADS = 2
A_HEAD_DIM = 64
AXIAL_THETA = 10000.0

B_HEADS = 8
B_QK_DIM = 32
B_V_DIM = 64
B_ROT_DIM = 8
ROPE_THETA = 500000.0

C_HEADS = 8
C_Q_RANK = 384
C_KV_RANK = 256
C_NOPE_DIM = 64
C_ROPE_DIM = 32
C_V_DIM = 64
MLA_THETA = 10000.0

N_BRANCH = 3
BRANCH_W = 512
D_FF = 4 * D_MODEL

LANES = 128
LOG2E = math.log2(math.e)
VMEM_LIMIT = 52 * 1024 * 1024
OUT_GROUPS = BRANCH_W // LANES

OFF_AQ, OFF_AK, OFF_AV = 0, 512, 640
OFF_BQ, OFF_BK, OFF_BV = 768, 1280, 1792
OFF_CQ, OFF_CKV, OFF_CKR = 2304, 2688, 2944
PROJ_COLS = 3072

TM_PROJ = 256
TM_MERGE = 256
TM_MLP = 256
RC_GQA = 512
RC_DIFF = 256
RC_MLA = 256

A_HEAD_ORDER = (0, 4, 1, 5, 2, 6, 3, 7)


def _cparams(n_axes):
    return pltpu.CompilerParams(
        dimension_semantics=("parallel",) * n_axes,
        vmem_limit_bytes=VMEM_LIMIT,
    )


def _const_spec(shape):
    nd = len(shape)
    return pl.BlockSpec(shape, lambda *_: (0,) * nd)


def _rms(x, g):
    return x * lax.rsqrt(jnp.mean(x * x, axis=-1, keepdims=True) + EPS) * g


def _lane_ids(shape):
    return lax.broadcasted_iota(jnp.int32, shape, len(shape) - 1)


def _rope_block(x, cos, sin_signed, first, shift):
    up = pltpu.roll(x, LANES - shift, 1)
    dn = pltpu.roll(x, shift, 1)
    return x * cos + jnp.where(first, up, dn) * sin_signed


def _rope(x, cos, sin_signed, first, shift):
    blocks = [
        _rope_block(x[:, i * LANES:(i + 1) * LANES], cos, sin_signed, first, shift)
        for i in range(x.shape[1] // LANES)
    ]
    return blocks[0] if len(blocks) == 1 else jnp.concatenate(blocks, axis=1)


def _group_sum(x, gmat):
    hi = x.astype(BF16)
    lo = (x - hi.astype(F32)).astype(BF16)
    return (jnp.dot(hi, gmat, preferred_element_type=F32)
            + jnp.dot(lo, gmat, preferred_element_type=F32))


def _store_groups(ref, x):
    for g in range(x.shape[1] // LANES):
        ref[0, g] = x[:, g * LANES:(g + 1) * LANES].astype(ref.dtype)


def _store_groups_t(ref, x):
    for g in range(x.shape[1] // LANES):
        ref[0, g] = x[:, g * LANES:(g + 1) * LANES].T.astype(ref.dtype)


def _store_values(ref, v):
    ones = jnp.ones((v.shape[0], LANES), ref.dtype)
    for g in range(v.shape[1] // LANES):
        ref[0, g] = jnp.concatenate(
            [v[:, g * LANES:(g + 1) * LANES].astype(ref.dtype), ones], axis=1)


def _proj_kernel(x_ref, g1_ref, w_ref, aqg_ref, akg_ref, cqg_ref, ckvg_ref,
                 wuq_ref, wkn_ref, wv_ref, gmat_ref,
                 cosa_ref, sina_ref, cosb_ref, sinb_ref, cosc_ref, sinc_ref,
                 qa_ref, ka_ref, va_ref, qb_ref, kb_ref, vb_ref,
                 qc_ref, kc_ref, vc_ref):
    x = x_ref[...]
    h = _rms(x, g1_ref[...]).astype(BF16)
    p = jnp.dot(h, w_ref[...], preferred_element_type=F32)

    lane = _lane_ids((1, LANES))
    first_a = (lane % 32) < 16
    first_b = (lane % 32) < 4
    first_c = (lane >= 64) & (lane < 80)
    cosa, sina = cosa_ref[...], sina_ref[...]
    cosb, sinb = cosb_ref[...], sinb_ref[...]
    cosc, sinc = cosc_ref[...], sinc_ref[...]
    gmat = gmat_ref[...]

    def head_norm(t, g, gm):
        ms = _group_sum(t * t, gm) * (1.0 / A_HEAD_DIM)
        return t * lax.rsqrt(ms + EPS) * g

    aq = head_norm(p[:, OFF_AQ:OFF_AQ + 512], aqg_ref[...], gmat)
    _store_groups(qa_ref, _rope(aq, cosa, sina, first_a, 16) * (A_HEAD_DIM ** -0.5 * LOG2E))
    ak = head_norm(p[:, OFF_AK:OFF_AK + 128], akg_ref[...], gmat[:LANES, :LANES])
    _store_groups_t(ka_ref, _rope(ak, cosa, sina, first_a, 16))
    _store_values(va_ref, p[:, OFF_AV:OFF_AV + 128])

    _store_groups(qb_ref, _rope(p[:, OFF_BQ:OFF_BQ + 512], cosb, sinb, first_b, 4)
                  * (B_QK_DIM ** -0.5 * LOG2E))
    _store_groups_t(kb_ref, _rope(p[:, OFF_BK:OFF_BK + 512], cosb, sinb, first_b, 4))
    _store_values(vb_ref, p[:, OFF_BV:OFF_BV + 512])

    cq = _rms(p[:, OFF_CQ:OFF_CQ + C_Q_RANK], cqg_ref[...]).astype(BF16)
    q_up = jnp.dot(cq, wuq_ref[...], preferred_element_type=F32)
    _store_groups(qc_ref, _rope(q_up, cosc, sinc, first_c, 16)
                  * ((C_NOPE_DIM + C_ROPE_DIM) ** -0.5 * LOG2E))
    ckv = _rms(p[:, OFF_CKV:OFF_CKV + C_KV_RANK], ckvg_ref[...]).astype(BF16)
    k_nope = jnp.dot(ckv, wkn_ref[...], preferred_element_type=F32)
    k_rope = _rope(p[:, OFF_CKR:OFF_CKR + LANES], cosc, sinc, first_c, 16)
    _store_groups_t(kc_ref, k_nope + jnp.concatenate([k_rope] * C_HEADS, axis=1))
    _store_values(vc_ref, jnp.dot(ckv, wv_ref[...], preferred_element_type=F32))


def _proj_call(x2, g1, w_main, aqg, akg, cqg, ckvg, wuq, wkn, wv, gmat, tables, batch, seq):
    n = x2.shape[0]
    tm = TM_PROJ
    nt = seq // tm
    tab_spec = pl.BlockSpec((tm, LANES), lambda i: (i % nt, 0))

    def rows(g, w):
        return (jax.ShapeDtypeStruct((batch, g, seq, w), BF16),
                pl.BlockSpec((1, g, tm, w), lambda i: (i // nt, 0, i % nt, 0)))

    def cols(g):
        return (jax.ShapeDtypeStruct((batch, g, LANES, seq), BF16),
                pl.BlockSpec((1, g, LANES, tm), lambda i: (i // nt, 0, 0, i % nt)))

    outs = [rows(4, LANES), cols(1), rows(1, 2 * LANES),
            rows(4, LANES), cols(4), rows(4, 2 * LANES),
            rows(8, LANES), cols(8), rows(4, 2 * LANES)]
    return pl.pallas_call(
        _proj_kernel,
        grid=(n // tm,),
        in_specs=[pl.BlockSpec((tm, D_MODEL), lambda i: (i, 0)),
                  _const_spec(g1.shape), _const_spec(w_main.shape),
                  _const_spec(aqg.shape), _const_spec(akg.shape),
                  _const_spec(cqg.shape), _const_spec(ckvg.shape),
                  _const_spec(wuq.shape), _const_spec(wkn.shape), _const_spec(wv.shape),
                  _const_spec(gmat.shape)] + [tab_spec] * 6,
        out_specs=[o[1] for o in outs],
        out_shape=[o[0] for o in outs],
        compiler_params=_cparams(1),
        name="proj_prep",
    )(x2, g1, w_main, aqg, akg, cqg, ckvg, wuq, wkn, wv, gmat, *tables)


def _run_pipeline(n_items, stage1, stage2, stage3):
    assert n_items % 2 == 0
    i32 = jnp.int32
    stage1(i32(0), 0)
    stage1(i32(1), 1)
    stage2(0)

    def body(u, carry):
        t = 2 * u + 1
        stage1(t + 1, 0)
        stage2(1)
        stage3(t - 1, 0)
        stage1(t + 2, 1)
        stage2(0)
        stage3(t, 1)
        return carry

    lax.fori_loop(0, (n_items - 2) // 2, body, 0)
    stage2(1)
    stage3(i32(n_items - 2), 0)
    stage3(i32(n_items - 1), 1)


def _item_locator(n_rows, rows_per_item):
    n_chunks = n_rows // rows_per_item
    assert n_chunks & (n_chunks - 1) == 0
    shift = n_chunks.bit_length() - 1

    def locate(t):
        c = t & (n_chunks - 1)
        return t >> shift, pl.ds(pl.multiple_of(c * rows_per_item, rows_per_item),
                                 rows_per_item)

    return locate, OUT_GROUPS * n_chunks


def _attn_pipeline(rows_per_item, n_sub, get_qk, get_v, finalize, o_ref,
                   s_buf, m_buf, e_buf):
    locate, n_items = _item_locator(o_ref.shape[2], rows_per_item)

    def stage1(t, slot):
        g, rows = locate(t)
        for j in range(n_sub):
            q, kt = get_qk(g, rows, j)
            s = jnp.dot(q, kt, preferred_element_type=F32)
            s_buf[slot, j] = s
            m_buf[slot, j] = jnp.max(s, axis=-1, keepdims=True)

    def stage2(slot):
        for j in range(n_sub):
            e_buf[slot, j] = jnp.exp2(s_buf[slot, j] - m_buf[slot, j]).astype(BF16)

    def stage3(t, slot):
        g, rows = locate(t)
        outs = []
        for j in range(n_sub):
            r = jnp.dot(e_buf[slot, j], get_v(g, j), preferred_element_type=F32)
            outs.append(r[:, :LANES] / r[:, LANES:])
        o_ref[0, g, rows, :] = finalize(outs).astype(o_ref.dtype)

    _run_pipeline(n_items, stage1, stage2, stage3)


def _attn_a_kernel(q_ref, k_ref, v_ref, o_ref, s_buf, m_buf, e_buf, *, rc):
    lane = _lane_ids((1, LANES))
    low = lane < 64

    def get_qk(g, rows, j):
        q = q_ref[0, g, rows, :]
        zero = jnp.zeros_like(q)
        return (jnp.where(low, q, zero) if j == 0 else jnp.where(low, zero, q)), k_ref[0, 0]

    def get_v(g, j):
        return v_ref[0, 0]

    def finalize(outs):
        return jnp.where(low, outs[0], outs[1])

    _attn_pipeline(rc, 2, get_qk, get_v, finalize, o_ref, s_buf, m_buf, e_buf)


def _attn_b_kernel(q_ref, k_ref, v_ref, lam_ref, linit_ref, subg_ref, o_ref,
                   s_buf, m_buf, e_buf, *, rc):
    lane = _lane_ids((1, LANES))
    lf = lam_ref[...]
    lam = (jnp.exp(jnp.sum(lf[0:1] * lf[1:2], axis=-1, keepdims=True))
           - jnp.exp(jnp.sum(lf[2:3] * lf[3:4], axis=-1, keepdims=True))
           + linit_ref[:, 0:1])
    post = subg_ref[...] * (1.0 - linit_ref[...])
    low = lane < 64

    def get_qk(g, rows, j):
        q = q_ref[0, g, rows, :]
        sel = (lane >= j * B_QK_DIM) & (lane < (j + 1) * B_QK_DIM)
        return jnp.where(sel, q, jnp.zeros_like(q)), k_ref[0, g]

    def get_v(g, j):
        return v_ref[0, g]

    def finalize(outs):
        heads = []
        for hh in range(2):
            o = outs[2 * hh] - lam * outs[2 * hh + 1]
            in_head = low if hh == 0 else jnp.logical_not(low)
            ms = jnp.sum(jnp.where(in_head, o * o, 0.0), axis=-1,
                         keepdims=True) * (1.0 / B_V_DIM)
            heads.append(o * lax.rsqrt(ms + EPS))
        return jnp.where(low, heads[0], heads[1]) * post

    _attn_pipeline(rc, 4, get_qk, get_v, finalize, o_ref, s_buf, m_buf, e_buf)


def _attn_c_kernel(q_ref, k_ref, v_ref, o_ref, s_buf, m_buf, e_buf, *, rc):
    lane = _lane_ids((1, LANES))
    low = lane < 64

    def get_qk(g, rows, j):
        return q_ref[0, 2 * g + j, rows, :], k_ref[0, 2 * g + j]

    def get_v(g, j):
        return v_ref[0, g]

    def finalize(outs):
        return jnp.where(low, outs[0], outs[1])

    _attn_pipeline(rc, 2, get_qk, get_v, finalize, o_ref, s_buf, m_buf, e_buf)


def _softmax_scratch(n_sub, rc, s):
    return [pltpu.VMEM((2, n_sub, rc, s), F32),
            pltpu.VMEM((2, n_sub, rc, 1), F32),
            pltpu.VMEM((2, n_sub, rc, s), BF16)]


def _attn_call(kernel, name, q, kt, v, extra, scratch, rc):
    b, _, s, _ = q.shape

    def batch_spec(a):
        return pl.BlockSpec((1,) + a.shape[1:], lambda bi: (bi, 0, 0, 0))

    return pl.pallas_call(
        functools.partial(kernel, rc=rc),
        grid=(b,),
        in_specs=[batch_spec(q), batch_spec(kt), batch_spec(v)]
        + [_const_spec(e.shape) for e in extra],
        out_specs=pl.BlockSpec((1, OUT_GROUPS, s, LANES), lambda bi: (bi, 0, 0, 0)),
        out_shape=jax.ShapeDtypeStruct((b, OUT_GROUPS, s, LANES), BF16),
        scratch_shapes=scratch,
        compiler_params=_cparams(1),
        name=name,
    )(q, kt, v, *extra)


def _merge_kernel(x_ref, g1_ref, ya_ref, yb_ref, yc_ref, wg_ref, wb_ref, wo_ref, o_ref):
    x = x_ref[...]
    h = _rms(x, g1_ref[...]).astype(BF16)
    merged = None
    for n, y_ref in enumerate((ya_ref, yb_ref, yc_ref)):
        logits = jnp.dot(h, wg_ref[:, n * D_MODEL:(n + 1) * D_MODEL],
                         preferred_element_type=F32)
        y = jnp.concatenate([y_ref[0, g] for g in range(OUT_GROUPS)], axis=1)
        z = jnp.dot(y, wb_ref[n], preferred_element_type=F32)
        term = jax.nn.sigmoid(logits) * z
        merged = term if merged is None else merged + term
    o_ref[...] = x + jnp.dot(merged.astype(BF16), wo_ref[...],
                             preferred_element_type=F32)


def _merge_call(x2, g1, ya, yb, yc, wg, wb, wo):
    n = x2.shape[0]
    tm = TM_MERGE
    nt = ya.shape[2] // tm
    tok = pl.BlockSpec((tm, D_MODEL), lambda i: (i, 0))
    y_spec = pl.BlockSpec((1, OUT_GROUPS, tm, LANES), lambda i: (i // nt, 0, i % nt, 0))
    return pl.pallas_call(
        _merge_kernel,
        grid=(n // tm,),
        in_specs=[tok, _const_spec(g1.shape), y_spec, y_spec, y_spec,
                  _const_spec(wg.shape), _const_spec(wb.shape), _const_spec(wo.shape)],
        out_specs=tok,
        out_shape=jax.ShapeDtypeStruct((n, D_MODEL), F32),
        compiler_params=_cparams(1),
        name="gated_merge",
    )(x2, g1, ya, yb, yc, wg, wb, wo)


FF_CHUNK = 1024


def _mlp_kernel(x_ref, g2_ref, w1_ref, w2_ref, gf_ref, o_ref, *, final):
    x = x_ref[...]
    h = _rms(x, g2_ref[...]).astype(BF16)
    acc = x
    for c in range(D_FF // FF_CHUNK):
        cols = pl.ds(c * FF_CHUNK, FF_CHUNK)
        hid = jnp.dot(h, w1_ref[:, cols], preferred_element_type=F32)
        hid = jnp.square(jnp.maximum(hid, 0.0)).astype(BF16)
        acc = acc + jnp.dot(hid, w2_ref[cols, :], preferred_element_type=F32)
    if final:
        acc = _rms(acc, gf_ref[...])
    o_ref[...] = acc


def _mlp_call(x2, g2, w1, w2, gf, final):
    n = x2.shape[0]
    tm = TM_MLP
    tok = pl.BlockSpec((tm, D_MODEL), lambda i: (i, 0))
    return pl.pallas_call(
        functools.partial(_mlp_kernel, final=final),
        grid=(n // tm,),
        in_specs=[tok, _const_spec(g2.shape), _const_spec(w1.shape),
                  _const_spec(w2.shape), _const_spec(gf.shape)],
        out_specs=tok,
        out_shape=jax.ShapeDtypeStruct((n, D_MODEL), F32),
        compiler_params=_cparams(1),
        name="mlp_final" if final else "mlp",
    )(x2, g2, w1, w2, gf)


def _angles(pos, dim, theta):
    inv_freq = theta ** (-jnp.arange(0, dim, 2, dtype=F32) / dim)
    return pos.astype(F32)[:, None] * inv_freq[None, :]


def _rope_tables(seq):
    t = jnp.arange(seq)
    half = A_HEAD_DIM // 2
    row_ang = _angles(t // GRID_W, half, AXIAL_THETA)
    col_ang = _angles(t % GRID_W, half, AXIAL_THETA)
    b_ang = _angles(t, B_ROT_DIM, ROPE_THETA)
    c_ang = _angles(t, C_ROPE_DIM, MLA_THETA)

    def pair(ang):
        c, s = jnp.cos(ang), jnp.sin(ang)
        return jnp.concatenate([c, c], 1), jnp.concatenate([-s, s], 1)

    one = lambda w: jnp.ones((seq, w), F32)
    zero = lambda w: jnp.zeros((seq, w), F32)

    rc, rs = pair(row_ang)
    cc, cs = pair(col_ang)
    cos_a = jnp.concatenate([rc, cc] * 2, 1)
    sin_a = jnp.concatenate([rs, cs] * 2, 1)

    bc, bs = pair(b_ang)
    cos_b = jnp.concatenate([bc, one(B_QK_DIM - B_ROT_DIM)] * 4, 1)
    sin_b = jnp.concatenate([bs, zero(B_QK_DIM - B_ROT_DIM)] * 4, 1)

    mc, ms = pair(c_ang)
    cos_c = jnp.concatenate([one(64), mc, one(32)], 1)
    sin_c = jnp.concatenate([zero(64), ms, zero(32)], 1)
    return cos_a, sin_a, cos_b, sin_b, cos_c, sin_c


def _layer_weights(l, w_in, c_w_uq, c_w_ukv, w_branch):
    w = w_in[l]
    a_q = w[:, 0:512].reshape(D_MODEL, A_HEADS, A_HEAD_DIM)
    a_q = a_q[:, jnp.array(A_HEAD_ORDER), :].reshape(D_MODEL, 512)
    c_kr = jnp.zeros((D_MODEL, LANES), F32).at[:, 64:96].set(w[:, 2944:2976])
    w_main = jnp.concatenate([a_q, w[:, 512:2944], c_kr], axis=1).astype(BF16)
    w_gate = w[:, 2976:].astype(BF16)

    uq = c_w_uq[l].reshape(C_Q_RANK, C_HEADS, C_NOPE_DIM + C_ROPE_DIM)
    uq = jnp.pad(uq, ((0, 0), (0, 0), (0, LANES - C_NOPE_DIM - C_ROPE_DIM)))
    wuq = uq.reshape(C_Q_RANK, C_HEADS * LANES).astype(BF16)
    ukv = c_w_ukv[l].reshape(C_KV_RANK, C_HEADS, C_NOPE_DIM + C_V_DIM)
    kn = jnp.pad(ukv[:, :, :C_NOPE_DIM], ((0, 0), (0, 0), (0, LANES - C_NOPE_DIM)))
    wkn = kn.reshape(C_KV_RANK, C_HEADS * LANES).astype(BF16)
    wv = ukv[:, :, C_NOPE_DIM:].reshape(C_KV_RANK, C_HEADS * C_V_DIM).astype(BF16)

    wb = w_branch[l]
    wb_a = wb[0].reshape(A_HEADS, A_HEAD_DIM, D_MODEL)[jnp.array(A_HEAD_ORDER)]
    wb = jnp.stack([wb_a.reshape(BRANCH_W, D_MODEL), wb[1], wb[2]]).astype(BF16)
    return w_main, w_gate, wuq, wkn, wv, wb


def kernel(x, ln1_g, w_in, a_q_norm, a_k_norm, b_lambda, b_subln, c_q_norm, c_kv_norm,
           c_w_uq, c_w_ukv, w_branch, w_out, ln2_g, w_ff1, w_ff2, final_g):
    b, s, d = x.shape
    n = b * s
    tables = _rope_tables(s)
    gidx = jnp.arange(BRANCH_W) // A_HEAD_DIM
    gmat = (gidx[:, None] == gidx[None, :]).astype(BF16)
    gf = final_g.reshape(1, d)

    x2 = x.reshape(n, d)
    for l in range(DEPTH):
        w_main, w_gate, wuq, wkn, wv, wb = _layer_weights(l, w_in, c_w_uq, c_w_ukv, w_branch)
        g1 = ln1_g[l].reshape(1, d)
        qa, ka, va, qb, kb, vb, qc, kc, vc = _proj_call(
            x2, g1, w_main,
            jnp.tile(a_q_norm[l], A_HEADS).reshape(1, 512),
            jnp.tile(a_k_norm[l], A_KV_HEADS).reshape(1, 128),
            c_q_norm[l].reshape(1, C_Q_RANK), c_kv_norm[l].reshape(1, C_KV_RANK),
            wuq, wkn, wv, gmat, tables, b, s)

        rc_a, rc_b, rc_c = (min(r, s) for r in (RC_GQA, RC_DIFF, RC_MLA))
        ya = _attn_call(_attn_a_kernel, "attn_gqa", qa, ka, va, (),
                        _softmax_scratch(2, rc_a, s), rc_a)
        lambda_init = 0.8 - 0.6 * math.exp(-0.3 * l)
        yb = _attn_call(_attn_b_kernel, "attn_diff", qb, kb, vb,
                        (b_lambda[l], jnp.full((1, LANES), lambda_init, F32),
                         jnp.tile(b_subln[l], 2).reshape(1, LANES)),
                        _softmax_scratch(4, rc_b, s), rc_b)
        yc = _attn_call(_attn_c_kernel, "attn_mla", qc, kc, vc, (),
                        _softmax_scratch(2, rc_c, s), rc_c)

        x2 = _merge_call(x2, g1, ya, yb, yc, w_gate, wb, w_out[l].astype(BF16))
        x2 = _mlp_call(x2, ln2_g[l].reshape(1, d), w_ff1[l].astype(BF16),
                       w_ff2[l].astype(BF16), gf, l == DEPTH - 1)
    return x2.reshape(b, s, d)
```

```python
import functools
import math

import jax
import jax.numpy as jnp
from jax import lax
from jax.experimental import pallas as pl
from jax.experimental.pallas import tpu as pltpu

F32 = jnp.float32
BF16 = jnp.bfloat16

D_MODEL = 1024
DEPTH = 4
GRID_W = 64
EPS = 1e-6

A_HEADS = 8
A_KV_HEADS = 2
A_HEAD_DIM = 64
AXIAL_THETA = 10000.0

B_HEADS = 8
B_QK_DIM = 32
B_V_DIM = 64
B_ROT_DIM = 8
ROPE_THETA = 500000.0

C_HEADS = 8
C_Q_RANK = 384
C_KV_RANK = 256
C_NOPE_DIM = 64
C_ROPE_DIM = 32
C_V_DIM = 64
MLA_THETA = 10000.0

N_BRANCH = 3
BRANCH_W = 512
D_FF = 4 * D_MODEL

LANES = 128
LOG2E = math.log2(math.e)
VMEM_LIMIT = 52 * 1024 * 1024
OUT_GROUPS = BRANCH_W // LANES

OFF_AQ, OFF_AK, OFF_AV = 0, 512, 640
OFF_BQ, OFF_BK, OFF_BV = 768, 1280, 1792
OFF_CQ, OFF_CKV, OFF_CKR = 2304, 2688, 2944
PROJ_COLS = 3072

TM_PROJ = 512
SUB_PROJ = 256
TM_MERGE = 256
TM_MLP = 256
RC_GQA = 512
RC_DIFF = 256
RC_MLA = 256

A_HEAD_ORDER = (0, 4, 1, 5, 2, 6, 3, 7)


def _cparams(n_axes):
    return pltpu.CompilerParams(
        dimension_semantics=("parallel",) * n_axes,
        vmem_limit_bytes=VMEM_LIMIT,
    )


def _const_spec(shape):
    nd = len(shape)
    return pl.BlockSpec(shape, lambda *_: (0,) * nd)


def _rms(x, g):
    return x * lax.rsqrt(jnp.mean(x * x, axis=-1, keepdims=True) + EPS) * g


def _lane_ids(shape):
    return lax.broadcasted_iota(jnp.int32, shape, len(shape) - 1)


def _rope_block(x, cos, sin_signed, first, shift):
    up = pltpu.roll(x, LANES - shift, 1)
    dn = pltpu.roll(x, shift, 1)
    return x * cos + jnp.where(first, up, dn) * sin_signed


def _rope(x, cos, sin_signed, first, shift):
    blocks = [
        _rope_block(x[:, i * LANES:(i + 1) * LANES], cos, sin_signed, first, shift)
        for i in range(x.shape[1] // LANES)
    ]
    return blocks[0] if len(blocks) == 1 else jnp.concatenate(blocks, axis=1)


def _group_sum(x, gmat):
    hi = x.astype(BF16)
    lo = (x - hi.astype(F32)).astype(BF16)
    return (jnp.dot(hi, gmat, preferred_element_type=F32)
            + jnp.dot(lo, gmat, preferred_element_type=F32))


def _store_groups(ref, rows, x):
    for g in range(x.shape[1] // LANES):
        ref[0, g, rows, :] = x[:, g * LANES:(g + 1) * LANES].astype(ref.dtype)


def _store_groups_t(ref, rows, x):
    for g in range(x.shape[1] // LANES):
        ref[0, g, :, rows] = x[:, g * LANES:(g + 1) * LANES].T.astype(ref.dtype)


def _store_values(ref, rows, v):
    ones = jnp.ones((v.shape[0], LANES), ref.dtype)
    for g in range(v.shape[1] // LANES):
        ref[0, g, rows, :] = jnp.concatenate(
            [v[:, g * LANES:(g + 1) * LANES].astype(ref.dtype), ones], axis=1)


def _proj_kernel(x_ref, g1_ref, w_ref, aqg_ref, akg_ref, cqg_ref, ckvg_ref,
                 wuq_ref, wkn_ref, wv_ref, gmat_ref,
                 cosa_ref, sina_ref, cosb_ref, sinb_ref, cosc_ref, sinc_ref,
                 qa_ref, ka_ref, va_ref, qb_ref, kb_ref, vb_ref,
                 qc_ref, kc_ref, vc_ref):
    lane = _lane_ids((1, LANES))
    first_a = (lane % 32) < 16
    first_b = (lane % 32) < 4
    first_c = (lane >= 64) & (lane < 80)
    gmat = gmat_ref[...]

    def head_norm(t, g, gm):
        ms = _group_sum(t * t, gm) * (1.0 / A_HEAD_DIM)
        return t * lax.rsqrt(ms + EPS) * g

    for sub in range(x_ref.shape[0] // SUB_PROJ):
        rows = pl.ds(sub * SUB_PROJ, SUB_PROJ)
        h = _rms(x_ref[rows, :], g1_ref[...]).astype(BF16)
        p = jnp.dot(h, w_ref[...], preferred_element_type=F32)
        cosa, sina = cosa_ref[rows, :], sina_ref[rows, :]
        cosb, sinb = cosb_ref[rows, :], sinb_ref[rows, :]
        cosc, sinc = cosc_ref[rows, :], sinc_ref[rows, :]

        aq = head_norm(p[:, OFF_AQ:OFF_AQ + 512], aqg_ref[...], gmat)
        _store_groups(qa_ref, rows, _rope(aq, cosa, sina, first_a, 16)
                      * (A_HEAD_DIM ** -0.5 * LOG2E))
        ak = head_norm(p[:, OFF_AK:OFF_AK + 128], akg_ref[...], gmat[:LANES, :LANES])
        _store_groups_t(ka_ref, rows, _rope(ak, cosa, sina, first_a, 16))
        _store_values(va_ref, rows, p[:, OFF_AV:OFF_AV + 128])

        _store_groups(qb_ref, rows, _rope(p[:, OFF_BQ:OFF_BQ + 512], cosb, sinb, first_b, 4)
                      * (B_QK_DIM ** -0.5 * LOG2E))
        _store_groups_t(kb_ref, rows,
                        _rope(p[:, OFF_BK:OFF_BK + 512], cosb, sinb, first_b, 4))
        _store_values(vb_ref, rows, p[:, OFF_BV:OFF_BV + 512])

        cq = _rms(p[:, OFF_CQ:OFF_CQ + C_Q_RANK], cqg_ref[...]).astype(BF16)
        q_up = jnp.dot(cq, wuq_ref[...], preferred_element_type=F32)
        _store_groups(qc_ref, rows, _rope(q_up, cosc, sinc, first_c, 16)
                      * ((C_NOPE_DIM + C_ROPE_DIM) ** -0.5 * LOG2E))
        ckv = _rms(p[:, OFF_CKV:OFF_CKV + C_KV_RANK], ckvg_ref[...]).astype(BF16)
        k_nope = jnp.dot(ckv, wkn_ref[...], preferred_element_type=F32)
        k_rope = _rope(p[:, OFF_CKR:OFF_CKR + LANES], cosc, sinc, first_c, 16)
        _store_groups_t(kc_ref, rows,
                        k_nope + jnp.concatenate([k_rope] * C_HEADS, axis=1))
        _store_values(vc_ref, rows, jnp.dot(ckv, wv_ref[...], preferred_element_type=F32))


def _proj_call(x2, g1, w_main, aqg, akg, cqg, ckvg, wuq, wkn, wv, gmat, tables, batch, seq):
    n = x2.shape[0]
    tm = TM_PROJ
    nt = seq // tm
    tab_spec = pl.BlockSpec((tm, LANES), lambda i: (i % nt, 0))

    def rows(g, w):
        return (jax.ShapeDtypeStruct((batch, g, seq, w), BF16),
                pl.BlockSpec((1, g, tm, w), lambda i: (i // nt, 0, i % nt, 0)))

    def cols(g):
        return (jax.ShapeDtypeStruct((batch, g, LANES, seq), BF16),
                pl.BlockSpec((1, g, LANES, tm), lambda i: (i // nt, 0, 0, i % nt)))

    outs = [rows(4, LANES), cols(1), rows(1, 2 * LANES),
            rows(4, LANES), cols(4), rows(4, 2 * LANES),
            rows(8, LANES), cols(8), rows(4, 2 * LANES)]
    return pl.pallas_call(
        _proj_kernel,
        grid=(n // tm,),
        in_specs=[pl.BlockSpec((tm, D_MODEL), lambda i: (i, 0)),
                  _const_spec(g1.shape), _const_spec(w_main.shape),
                  _const_spec(aqg.shape), _const_spec(akg.shape),
                  _const_spec(cqg.shape), _const_spec(ckvg.shape),
                  _const_spec(wuq.shape), _const_spec(wkn.shape), _const_spec(wv.shape),
                  _const_spec(gmat.shape)] + [tab_spec] * 6,
        out_specs=[o[1] for o in outs],
        out_shape=[o[0] for o in outs],
        compiler_params=_cparams(1),
        name="proj_prep",
    )(x2, g1, w_main, aqg, akg, cqg, ckvg, wuq, wkn, wv, gmat, *tables)


def _run_pipeline(n_items, stage1, stage2, stage3):
    assert n_items % 2 == 0
    i32 = jnp.int32
    stage1(i32(0), 0)
    stage1(i32(1), 1)
    stage2(0)

    def body(u, carry):
        t = 2 * u + 1
        stage1(t + 1, 0)
        stage2(1)
        stage3(t - 1, 0)
        stage1(t + 2, 1)
        stage2(0)
        stage3(t, 1)
        return carry

    lax.fori_loop(0, (n_items - 2) // 2, body, 0)
    stage2(1)
    stage3(i32(n_items - 2), 0)
    stage3(i32(n_items - 1), 1)


def _item_locator(n_rows, rows_per_item):
    n_chunks = n_rows // rows_per_item
    assert n_chunks & (n_chunks - 1) == 0
    shift = n_chunks.bit_length() - 1

    def locate(t):
        c = t & (n_chunks - 1)
        return t >> shift, pl.ds(pl.multiple_of(c * rows_per_item, rows_per_item),
                                 rows_per_item)

    return locate, OUT_GROUPS * n_chunks


def _attn_pipeline(rows_per_item, n_sub, get_qk, get_v, finalize, o_ref,
                   s_buf, m_buf, e_buf):
    locate, n_items = _item_locator(o_ref.shape[2], rows_per_item)

    def stage1(t, slot):
        g, rows = locate(t)
        for j in range(n_sub):
            q, kt = get_qk(g, rows, j)
            s = jnp.dot(q, kt, preferred_element_type=F32)
            s_buf[slot, j] = s
            m_buf[slot, j] = jnp.max(s, axis=-1, keepdims=True)

    def stage2(slot):
        for j in range(n_sub):
            e_buf[slot, j] = jnp.exp2(s_buf[slot, j] - m_buf[slot, j]).astype(BF16)

    def stage3(t, slot):
        g, rows = locate(t)
        outs = []
        for j in range(n_sub):
            r = jnp.dot(e_buf[slot, j], get_v(g, j), preferred_element_type=F32)
            outs.append(r[:, :LANES] / r[:, LANES:])
        o_ref[0, g, rows, :] = finalize(outs).astype(o_ref.dtype)

    _run_pipeline(n_items, stage1, stage2, stage3)


def _attn_a_kernel(q_ref, k_ref, v_ref, o_ref, s_buf, m_buf, e_buf, *, rc):
    lane = _lane_ids((1, LANES))
    low = lane < 64

    def get_qk(g, rows, j):
        q = q_ref[0, g, rows, :]
        zero = jnp.zeros_like(q)
        return (jnp.where(low, q, zero) if j == 0 else jnp.where(low, zero, q)), k_ref[0, 0]

    def get_v(g, j):
        return v_ref[0, 0]

    def finalize(outs):
        return jnp.where(low, outs[0], outs[1])

    _attn_pipeline(rc, 2, get_qk, get_v, finalize, o_ref, s_buf, m_buf, e_buf)


def _attn_b_kernel(q_ref, k_ref, v_ref, lam_ref, linit_ref, subg_ref, o_ref,
                   s_buf, m_buf, e_buf, *, rc):
    lane = _lane_ids((1, LANES))
    lf = lam_ref[...]
    lam = (jnp.exp(jnp.sum(lf[0:1] * lf[1:2], axis=-1, keepdims=True))
           - jnp.exp(jnp.sum(lf[2:3] * lf[3:4], axis=-1, keepdims=True))
           + linit_ref[:, 0:1])
    post = subg_ref[...] * (1.0 - linit_ref[...])
    low = lane < 64

    def get_qk(g, rows, j):
        q = q_ref[0, g, rows, :]
        sel = (lane >= j * B_QK_DIM) & (lane < (j + 1) * B_QK_DIM)
        return jnp.where(sel, q, jnp.zeros_like(q)), k_ref[0, g]

    def get_v(g, j):
        return v_ref[0, g]

    def finalize(outs):
        heads = []
        for hh in range(2):
            o = outs[2 * hh] - lam * outs[2 * hh + 1]
            in_head = low if hh == 0 else jnp.logical_not(low)
            ms = jnp.sum(jnp.where(in_head, o * o, 0.0), axis=-1,
                         keepdims=True) * (1.0 / B_V_DIM)
            heads.append(o * lax.rsqrt(ms + EPS))
        return jnp.where(low, heads[0], heads[1]) * post

    _attn_pipeline(rc, 4, get_qk, get_v, finalize, o_ref, s_buf, m_buf, e_buf)


def _attn_c_kernel(q_ref, k_ref, v_ref, o_ref, s_buf, m_buf, e_buf, *, rc):
    lane = _lane_ids((1, LANES))
    low = lane < 64

    def get_qk(g, rows, j):
        return q_ref[0, 2 * g + j, rows, :], k_ref[0, 2 * g + j]

    def get_v(g, j):
        return v_ref[0, g]

    def finalize(outs):
        return jnp.where(low, outs[0], outs[1])

    _attn_pipeline(rc, 2, get_qk, get_v, finalize, o_ref, s_buf, m_buf, e_buf)


def _softmax_scratch(n_sub, rc, s):
    return [pltpu.VMEM((2, n_sub, rc, s), F32),
            pltpu.VMEM((2, n_sub, rc, 1), F32),
            pltpu.VMEM((2, n_sub, rc, s), BF16)]


def _attn_call(kernel, name, q, kt, v, extra, n_sub, rc):
    b, _, s, _ = q.shape
    rc = min(rc, s)

    def batch_spec(a):
        return pl.BlockSpec((1,) + a.shape[1:], lambda bi: (bi, 0, 0, 0))

    return pl.pallas_call(
        functools.partial(kernel, rc=rc),
        grid=(b,),
        in_specs=[batch_spec(q), batch_spec(kt), batch_spec(v)]
        + [_const_spec(e.shape) for e in extra],
        out_specs=pl.BlockSpec((1, OUT_GROUPS, s, LANES), lambda bi: (bi, 0, 0, 0)),
        out_shape=jax.ShapeDtypeStruct((b, OUT_GROUPS, s, LANES), BF16),
        scratch_shapes=_softmax_scratch(n_sub, rc, s),
        compiler_params=_cparams(1),
        name=name,
    )(q, kt, v, *extra)


def _merge_kernel(x_ref, g1_ref, ya_ref, yb_ref, yc_ref, wg_ref, wb_ref, wo_ref, o_ref):
    x = x_ref[...]
    h = _rms(x, g1_ref[...]).astype(BF16)
    merged = None
    for n, y_ref in enumerate((ya_ref, yb_ref, yc_ref)):
        logits = jnp.dot(h, wg_ref[:, n * D_MODEL:(n + 1) * D_MODEL],
                         preferred_element_type=F32)
        y = jnp.concatenate([y_ref[0, g] for g in range(OUT_GROUPS)], axis=1)
        z = jnp.dot(y, wb_ref[n], preferred_element_type=F32)
        term = jax.nn.sigmoid(logits) * z
        merged = term if merged is None else merged + term
    o_ref[...] = x + jnp.dot(merged.astype(BF16), wo_ref[...],
                             preferred_element_type=F32)


def _merge_call(x2, g1, ya, yb, yc, wg, wb, wo):
    n = x2.shape[0]
    tm = TM_MERGE
    nt = ya.shape[2] // tm
    tok = pl.BlockSpec((tm, D_MODEL), lambda i: (i, 0))
    y_spec = pl.BlockSpec((1, OUT_GROUPS, tm, LANES), lambda i: (i // nt, 0, i % nt, 0))
    return pl.pallas_call(
        _merge_kernel,
        grid=(n // tm,),
        in_specs=[tok, _const_spec(g1.shape), y_spec, y_spec, y_spec,
                  _const_spec(wg.shape), _const_spec(wb.shape), _const_spec(wo.shape)],
        out_specs=tok,
        out_shape=jax.ShapeDtypeStruct((n, D_MODEL), F32),
        compiler_params=_cparams(1),
        name="gated_merge",
    )(x2, g1, ya, yb, yc, wg, wb, wo)


FF_CHUNK = 1024


def _mlp_kernel(x_ref, g2_ref, w1_ref, w2_ref, gf_ref, o_ref, *, final):
    x = x_ref[...]
    h = _rms(x, g2_ref[...]).astype(BF16)
    acc = x
    for c in range(D_FF // FF_CHUNK):
        cols = pl.ds(c * FF_CHUNK, FF_CHUNK)
        hid = jnp.dot(h, w1_ref[:, cols], preferred_element_type=F32)
        hid = jnp.square(jnp.maximum(hid, 0.0)).astype(BF16)
        acc = acc + jnp.dot(hid, w2_ref[cols, :], preferred_element_type=F32)
    if final:
        acc = _rms(acc, gf_ref[...])
    o_ref[...] = acc


def _mlp_call(x2, g2, w1, w2, gf, final):
    n = x2.shape[0]
    tm = TM_MLP
    tok = pl.BlockSpec((tm, D_MODEL), lambda i: (i, 0))
    return pl.pallas_call(
        functools.partial(_mlp_kernel, final=final),
        grid=(n // tm,),
        in_specs=[tok, _const_spec(g2.shape), _const_spec(w1.shape),
                  _const_spec(w2.shape), _const_spec(gf.shape)],
        out_specs=tok,
        out_shape=jax.ShapeDtypeStruct((n, D_MODEL), F32),
        compiler_params=_cparams(1),
        name="mlp_final" if final else "mlp",
    )(x2, g2, w1, w2, gf)


def _angles(pos, dim, theta):
    inv_freq = theta ** (-jnp.arange(0, dim, 2, dtype=F32) / dim)
    return pos.astype(F32)[:, None] * inv_freq[None, :]


def _rope_tables(seq):
    t = jnp.arange(seq)
    half = A_HEAD_DIM // 2
    row_ang = _angles(t // GRID_W, half, AXIAL_THETA)
    col_ang = _angles(t % GRID_W, half, AXIAL_THETA)
    b_ang = _angles(t, B_ROT_DIM, ROPE_THETA)
    c_ang = _angles(t, C_ROPE_DIM, MLA_THETA)

    def pair(ang):
        c, s = jnp.cos(ang), jnp.sin(ang)
        return jnp.concatenate([c, c], 1), jnp.concatenate([-s, s], 1)

    one = lambda w: jnp.ones((seq, w), F32)
    zero = lambda w: jnp.zeros((seq, w), F32)

    rc, rs = pair(row_ang)
    cc, cs = pair(col_ang)
    cos_a = jnp.concatenate([rc, cc] * 2, 1)
    sin_a = jnp.concatenate([rs, cs] * 2, 1)

    bc, bs = pair(b_ang)
    cos_b = jnp.concatenate([bc, one(B_QK_DIM - B_ROT_DIM)] * 4, 1)
    sin_b = jnp.concatenate([bs, zero(B_QK_DIM - B_ROT_DIM)] * 4, 1)

    mc, ms = pair(c_ang)
    cos_c = jnp.concatenate([one(64), mc, one(32)], 1)
    sin_c = jnp.concatenate([zero(64), ms, zero(32)], 1)
    return cos_a, sin_a, cos_b, sin_b, cos_c, sin_c


def _layer_weights(l, w_in, c_w_uq, c_w_ukv, w_branch):
    w = w_in[l]
    a_q = w[:, 0:512].reshape(D_MODEL, A_HEADS, A_HEAD_DIM)
    a_q = a_q[:, jnp.array(A_HEAD_ORDER), :].reshape(D_MODEL, 512)
    c_kr = jnp.zeros((D_MODEL, LANES), F32).at[:, 64:96].set(w[:, 2944:2976])
    w_main = jnp.concatenate([a_q, w[:, 512:2944], c_kr], axis=1).astype(BF16)
    w_gate = w[:, 2976:].astype(BF16)

    uq = c_w_uq[l].reshape(C_Q_RANK, C_HEADS, C_NOPE_DIM + C_ROPE_DIM)
    uq = jnp.pad(uq, ((0, 0), (0, 0), (0, LANES - C_NOPE_DIM - C_ROPE_DIM)))
    wuq = uq.reshape(C_Q_RANK, C_HEADS * LANES).astype(BF16)
    ukv = c_w_ukv[l].reshape(C_KV_RANK, C_HEADS, C_NOPE_DIM + C_V_DIM)
    kn = jnp.pad(ukv[:, :, :C_NOPE_DIM], ((0, 0), (0, 0), (0, LANES - C_NOPE_DIM)))
    wkn = kn.reshape(C_KV_RANK, C_HEADS * LANES).astype(BF16)
    wv = ukv[:, :, C_NOPE_DIM:].reshape(C_KV_RANK, C_HEADS * C_V_DIM).astype(BF16)

    wb = w_branch[l]
    wb_a = wb[0].reshape(A_HEADS, A_HEAD_DIM, D_MODEL)[jnp.array(A_HEAD_ORDER)]
    wb = jnp.stack([wb_a.reshape(BRANCH_W, D_MODEL), wb[1], wb[2]]).astype(BF16)
    return w_main, w_gate, wuq, wkn, wv, wb


def kernel(x, ln1_g, w_in, a_q_norm, a_k_norm, b_lambda, b_subln, c_q_norm, c_kv_norm,
           c_w_uq, c_w_ukv, w_branch, w_out, ln2_g, w_ff1, w_ff2, final_g):
    b, s, d = x.shape
    n = b * s
    tables = _rope_tables(s)
    gidx = jnp.arange(BRANCH_W) // A_HEAD_DIM
    gmat = (gidx[:, None] == gidx[None, :]).astype(BF16)
    gf = final_g.reshape(1, d)

    x2 = x.reshape(n, d)
    for l in range(DEPTH):
        w_main, w_gate, wuq, wkn, wv, wb = _layer_weights(l, w_in, c_w_uq, c_w_ukv, w_branch)
        g1 = ln1_g[l].reshape(1, d)
        qa, ka, va, qb, kb, vb, qc, kc, vc = _proj_call(
            x2, g1, w_main,
            jnp.tile(a_q_norm[l], A_HEADS).reshape(1, 512),
            jnp.tile(a_k_norm[l], A_KV_HEADS).reshape(1, 128),
            c_q_norm[l].reshape(1, C_Q_RANK), c_kv_norm[l].reshape(1, C_KV_RANK),
            wuq, wkn, wv, gmat, tables, b, s)

        ya = _attn_call(_attn_a_kernel, "attn_gqa", qa, ka, va, (), 2, RC_GQA)
        lambda_init = 0.8 - 0.6 * math.exp(-0.3 * l)
        yb = _attn_call(_attn_b_kernel, "attn_diff", qb, kb, vb,
                        (b_lambda[l], jnp.full((1, LANES), lambda_init, F32),
                         jnp.tile(b_subln[l], 2).reshape(1, LANES)), 4, RC_DIFF)
        yc = _attn_call(_attn_c_kernel, "attn_mla", qc, kc, vc, (), 2, RC_MLA)

        x2 = _merge_call(x2, g1, ya, yb, yc, w_gate, wb, w_out[l].astype(BF16))
        x2 = _mlp_call(x2, ln2_g[l].reshape(1, d), w_ff1[l].astype(BF16),
                       w_ff2[l].astype(BF16), gf, l == DEPTH - 1)
    return x2.reshape(b, s, d)
```

```python
import functools
import math

import jax
import jax.numpy as jnp
from jax import lax
from jax.experimental import pallas as pl
from jax.experimental.pallas import tpu as pltpu

F32 = jnp.float32
BF16 = jnp.bfloat16

D_MODEL = 1024
DEPTH = 4
GRID_W = 64
EPS = 1e-6

A_HEADS = 8
A_KV_HEADS = 2
A_HEAD_DIM = 64
AXIAL_THETA = 10000.0

B_HEADS = 8
B_QK_DIM = 32
B_V_DIM = 64
B_ROT_DIM = 8
ROPE_THETA = 500000.0

C_HEADS = 8
C_Q_RANK = 384
C_KV_RANK = 256
C_NOPE_DIM = 64
C_ROPE_DIM = 32
C_V_DIM = 64
MLA_THETA = 10000.0

N_BRANCH = 3
BRANCH_W = 512
D_FF = 4 * D_MODEL

LANES = 128
LOG2E = math.log2(math.e)
VMEM_LIMIT = 52 * 1024 * 1024
OUT_GROUPS = BRANCH_W // LANES

OFF_AQ, OFF_AK, OFF_AV = 0, 512, 640
OFF_BQ, OFF_BK, OFF_BV = 768, 1280, 1792
OFF_CQ, OFF_CKV, OFF_CKR = 2304, 2688, 2944
PROJ_COLS = 3072

TM_PROJ = 512
SUB_PROJ = 256
TM_MERGE = 512
TM_MLP = 512
RC_GQA = 512
RC_DIFF = 256
RC_MLA = 256

A_HEAD_ORDER = (0, 4, 1, 5, 2, 6, 3, 7)


def _cparams(n_axes):
    return pltpu.CompilerParams(
        dimension_semantics=("parallel",) * n_axes,
        vmem_limit_bytes=VMEM_LIMIT,
    )


def _const_spec(shape):
    nd = len(shape)
    return pl.BlockSpec(shape, lambda *_: (0,) * nd, pipeline_mode=pl.Buffered(1))


def _rms(x, g):
    return x * lax.rsqrt(jnp.mean(x * x, axis=-1, keepdims=True) + EPS) * g


def _lane_ids(shape):
    return lax.broadcasted_iota(jnp.int32, shape, len(shape) - 1)


def _rope_block(x, cos, sin_signed, first, shift):
    up = pltpu.roll(x, LANES - shift, 1)
    dn = pltpu.roll(x, shift, 1)
    return x * cos + jnp.where(first, up, dn) * sin_signed


def _rope(x, cos, sin_signed, first, shift):
    blocks = [
        _rope_block(x[:, i * LANES:(i + 1) * LANES], cos, sin_signed, first, shift)
        for i in range(x.shape[1] // LANES)
    ]
    return blocks[0] if len(blocks) == 1 else jnp.concatenate(blocks, axis=1)


def _group_sum(x, gmat):
    hi = x.astype(BF16)
    lo = (x - hi.astype(F32)).astype(BF16)
    return (jnp.dot(hi, gmat, preferred_element_type=F32)
            + jnp.dot(lo, gmat, preferred_element_type=F32))


def _store_groups(ref, rows, x):
    for g in range(x.shape[1] // LANES):
        ref[0, g, rows, :] = x[:, g * LANES:(g + 1) * LANES].astype(ref.dtype)


def _store_groups_t(ref, rows, x):
    for g in range(x.shape[1] // LANES):
        ref[0, g, :, rows] = x[:, g * LANES:(g + 1) * LANES].T.astype(ref.dtype)


def _store_values(ref, rows, v):
    ones = jnp.ones((v.shape[0], LANES), ref.dtype)
    for g in range(v.shape[1] // LANES):
        ref[0, g, rows, :] = jnp.concatenate(
            [v[:, g * LANES:(g + 1) * LANES].astype(ref.dtype), ones], axis=1)


def _proj_kernel(x_ref, g1_ref, w_ref, aqg_ref, akg_ref, cqg_ref, ckvg_ref,
                 wuq_ref, wkn_ref, wv_ref, gmat_ref,
                 cosa_ref, sina_ref, cosb_ref, sinb_ref, cosc_ref, sinc_ref,
                 qa_ref, ka_ref, va_ref, qb_ref, kb_ref, vb_ref,
                 qc_ref, kc_ref, vc_ref):
    lane = _lane_ids((1, LANES))
    first_a = (lane % 32) < 16
    first_b = (lane % 32) < 4
    first_c = (lane >= 64) & (lane < 80)
    gmat = gmat_ref[...]

    def head_norm(t, g, gm):
        ms = _group_sum(t * t, gm) * (1.0 / A_HEAD_DIM)
        return t * lax.rsqrt(ms + EPS) * g

    for sub in range(x_ref.shape[0] // SUB_PROJ):
        rows = pl.ds(sub * SUB_PROJ, SUB_PROJ)
        h = _rms(x_ref[rows, :], g1_ref[...]).astype(BF16)
        p = jnp.dot(h, w_ref[...], preferred_element_type=F32)
        cosa, sina = cosa_ref[rows, :], sina_ref[rows, :]
        cosb, sinb = cosb_ref[rows, :], sinb_ref[rows, :]
        cosc, sinc = cosc_ref[rows, :], sinc_ref[rows, :]

        aq = head_norm(p[:, OFF_AQ:OFF_AQ + 512], aqg_ref[...], gmat)
        _store_groups(qa_ref, rows, _rope(aq, cosa, sina, first_a, 16)
                      * (A_HEAD_DIM ** -0.5 * LOG2E))
        ak = head_norm(p[:, OFF_AK:OFF_AK + 128], akg_ref[...], gmat[:LANES, :LANES])
        _store_groups_t(ka_ref, rows, _rope(ak, cosa, sina, first_a, 16))
        _store_values(va_ref, rows, p[:, OFF_AV:OFF_AV + 128])

        _store_groups(qb_ref, rows, _rope(p[:, OFF_BQ:OFF_BQ + 512], cosb, sinb, first_b, 4)
                      * (B_QK_DIM ** -0.5 * LOG2E))
        _store_groups_t(kb_ref, rows,
                        _rope(p[:, OFF_BK:OFF_BK + 512], cosb, sinb, first_b, 4))
        _store_values(vb_ref, rows, p[:, OFF_BV:OFF_BV + 512])

        cq = _rms(p[:, OFF_CQ:OFF_CQ + C_Q_RANK], cqg_ref[...]).astype(BF16)
        q_up = jnp.dot(cq, wuq_ref[...], preferred_element_type=F32)
        _store_groups(qc_ref, rows, _rope(q_up, cosc, sinc, first_c, 16)
                      * ((C_NOPE_DIM + C_ROPE_DIM) ** -0.5 * LOG2E))
        ckv = _rms(p[:, OFF_CKV:OFF_CKV + C_KV_RANK], ckvg_ref[...]).astype(BF16)
        k_nope = jnp.dot(ckv, wkn_ref[...], preferred_element_type=F32)
        k_rope = _rope(p[:, OFF_CKR:OFF_CKR + LANES], cosc, sinc, first_c, 16)
        _store_groups_t(kc_ref, rows,
                        k_nope + jnp.concatenate([k_rope] * C_HEADS, axis=1))
        _store_values(vc_ref, rows, jnp.dot(ckv, wv_ref[...], preferred_element_type=F32))


def _proj_call(x2, g1, w_main, aqg, akg, cqg, ckvg, wuq, wkn, wv, gmat, tables, batch, seq):
    n = x2.shape[0]
    tm = TM_PROJ
    nt = seq // tm
    tab_spec = pl.BlockSpec((tm, LANES), lambda i: (i % nt, 0))

    def rows(g, w):
        return (jax.ShapeDtypeStruct((batch, g, seq, w), BF16),
                pl.BlockSpec((1, g, tm, w), lambda i: (i // nt, 0, i % nt, 0)))

    def cols(g):
        return (jax.ShapeDtypeStruct((batch, g, LANES, seq), BF16),
                pl.BlockSpec((1, g, LANES, tm), lambda i: (i // nt, 0, 0, i % nt)))

    outs = [rows(4, LANES), cols(1), rows(1, 2 * LANES),
            rows(4, LANES), cols(4), rows(4, 2 * LANES),
            rows(8, LANES), cols(8), rows(4, 2 * LANES)]
    return pl.pallas_call(
        _proj_kernel,
        grid=(n // tm,),
        in_specs=[pl.BlockSpec((tm, D_MODEL), lambda i: (i, 0)),
                  _const_spec(g1.shape), _const_spec(w_main.shape),
                  _const_spec(aqg.shape), _const_spec(akg.shape),
                  _const_spec(cqg.shape), _const_spec(ckvg.shape),
                  _const_spec(wuq.shape), _const_spec(wkn.shape), _const_spec(wv.shape),
                  _const_spec(gmat.shape)] + [tab_spec] * 6,
        out_specs=[o[1] for o in outs],
        out_shape=[o[0] for o in outs],
        compiler_params=_cparams(1),
        name="proj_prep",
    )(x2, g1, w_main, aqg, akg, cqg, ckvg, wuq, wkn, wv, gmat, *tables)


def _run_pipeline(n_items, stage1, stage2, stage3):
    assert n_items % 2 == 0
    i32 = jnp.int32
    stage1(i32(0), 0)
    stage1(i32(1), 1)
    stage2(0)

    def body(u, carry):
        t = 2 * u + 1
        stage1(t + 1, 0)
        stage2(1)
        stage3(t - 1, 0)
        stage1(t + 2, 1)
        stage2(0)
        stage3(t, 1)
        return carry

    lax.fori_loop(0, (n_items - 2) // 2, body, 0)
    stage2(1)
    stage3(i32(n_items - 2), 0)
    stage3(i32(n_items - 1), 1)


def _item_locator(n_rows, rows_per_item):
    n_chunks = n_rows // rows_per_item
    assert n_chunks & (n_chunks - 1) == 0
    shift = n_chunks.bit_length() - 1

    def locate(t):
        c = t & (n_chunks - 1)
        return t >> shift, pl.ds(pl.multiple_of(c * rows_per_item, rows_per_item),
                                 rows_per_item)

    return locate, OUT_GROUPS * n_chunks


def _attn_pipeline(rows_per_item, n_sub, get_qk, get_v, finalize, o_ref,
                   s_buf, m_buf, e_buf):
    locate, n_items = _item_locator(o_ref.shape[2], rows_per_item)

    def stage1(t, slot):
        g, rows = locate(t)
        for j in range(n_sub):
            q, kt = get_qk(g, rows, j)
            s = jnp.dot(q, kt, preferred_element_type=F32)
            s_buf[slot, j] = s
            m_buf[slot, j] = jnp.max(s, axis=-1, keepdims=True)

    def stage2(slot):
        for j in range(n_sub):
            e_buf[slot, j] = jnp.exp2(s_buf[slot, j] - m_buf[slot, j]).astype(BF16)

    def stage3(t, slot):
        g, rows = locate(t)
        outs = []
        for j in range(n_sub):
            r = jnp.dot(e_buf[slot, j], get_v(g, j), preferred_element_type=F32)
            outs.append(r[:, :LANES] / r[:, LANES:])
        o_ref[0, g, rows, :] = finalize(outs).astype(o_ref.dtype)

    _run_pipeline(n_items, stage1, stage2, stage3)


def _attn_a_kernel(q_ref, k_ref, v_ref, o_ref, s_buf, m_buf, e_buf, *, rc):
    lane = _lane_ids((1, LANES))
    low = lane < 64

    def get_qk(g, rows, j):
        q = q_ref[0, g, rows, :]
        zero = jnp.zeros_like(q)
        return (jnp.where(low, q, zero) if j == 0 else jnp.where(low, zero, q)), k_ref[0, 0]

    def get_v(g, j):
        return v_ref[0, 0]

    def finalize(outs):
        return jnp.where(low, outs[0], outs[1])

    _attn_pipeline(rc, 2, get_qk, get_v, finalize, o_ref, s_buf, m_buf, e_buf)


def _attn_b_kernel(q_ref, k_ref, v_ref, lam_ref, linit_ref, subg_ref, o_ref,
                   s_buf, m_buf, e_buf, *, rc):
    lane = _lane_ids((1, LANES))
    lf = lam_ref[...]
    lam = (jnp.exp(jnp.sum(lf[0:1] * lf[1:2], axis=-1, keepdims=True))
           - jnp.exp(jnp.sum(lf[2:3] * lf[3:4], axis=-1, keepdims=True))
           + linit_ref[:, 0:1])
    post = subg_ref[...] * (1.0 - linit_ref[...])
    low = lane < 64

    def get_qk(g, rows, j):
        q = q_ref[0, g, rows, :]
        sel = (lane >= j * B_QK_DIM) & (lane < (j + 1) * B_QK_DIM)
        return jnp.where(sel, q, jnp.zeros_like(q)), k_ref[0, g]

    def get_v(g, j):
        return v_ref[0, g]

    def finalize(outs):
        heads = []
        for hh in range(2):
            o = outs[2 * hh] - lam * outs[2 * hh + 1]
            in_head = low if hh == 0 else jnp.logical_not(low)
            ms = jnp.sum(jnp.where(in_head, o * o, 0.0), axis=-1,
                         keepdims=True) * (1.0 / B_V_DIM)
            heads.append(o * lax.rsqrt(ms + EPS))
        return jnp.where(low, heads[0], heads[1]) * post

    _attn_pipeline(rc, 4, get_qk, get_v, finalize, o_ref, s_buf, m_buf, e_buf)


def _attn_c_kernel(q_ref, k_ref, v_ref, o_ref, s_buf, m_buf, e_buf, *, rc):
    lane = _lane_ids((1, LANES))
    low = lane < 64

    def get_qk(g, rows, j):
        return q_ref[0, 2 * g + j, rows, :], k_ref[0, 2 * g + j]

    def get_v(g, j):
        return v_ref[0, g]

    def finalize(outs):
        return jnp.where(low, outs[0], outs[1])

    _attn_pipeline(rc, 2, get_qk, get_v, finalize, o_ref, s_buf, m_buf, e_buf)


def _softmax_scratch(n_sub, rc, s):
    return [pltpu.VMEM((2, n_sub, rc, s), F32),
            pltpu.VMEM((2, n_sub, rc, 1), F32),
            pltpu.VMEM((2, n_sub, rc, s), BF16)]


def _attn_call(kernel, name, q, kt, v, extra, n_sub, rc):
    b, _, s, _ = q.shape
    rc = min(rc, s)

    def batch_spec(a):
        return pl.BlockSpec((1,) + a.shape[1:], lambda bi: (bi, 0, 0, 0))

    return pl.pallas_call(
        functools.partial(kernel, rc=rc),
        grid=(b,),
        in_specs=[batch_spec(q), batch_spec(kt), batch_spec(v)]
        + [_const_spec(e.shape) for e in extra],
        out_specs=pl.BlockSpec((1, OUT_GROUPS, s, LANES), lambda bi: (bi, 0, 0, 0)),
        out_shape=jax.ShapeDtypeStruct((b, OUT_GROUPS, s, LANES), BF16),
        scratch_shapes=_softmax_scratch(n_sub, rc, s),
        compiler_params=_cparams(1),
        name=name,
    )(q, kt, v, *extra)


def _merge_kernel(x_ref, g1_ref, ya_ref, yb_ref, yc_ref, wg_ref, wb_ref, wo_ref, o_ref):
    x = x_ref[...]
    h = _rms(x, g1_ref[...]).astype(BF16)
    merged = None
    for n, y_ref in enumerate((ya_ref, yb_ref, yc_ref)):
        logits = jnp.dot(h, wg_ref[:, n * D_MODEL:(n + 1) * D_MODEL],
                         preferred_element_type=F32)
        y = jnp.concatenate([y_ref[0, g] for g in range(OUT_GROUPS)], axis=1)
        z = jnp.dot(y, wb_ref[n], preferred_element_type=F32)
        term = jax.nn.sigmoid(logits) * z
        merged = term if merged is None else merged + term
    o_ref[...] = x + jnp.dot(merged.astype(BF16), wo_ref[...],
                             preferred_element_type=F32)


def _merge_call(x2, g1, ya, yb, yc, wg, wb, wo):
    n = x2.shape[0]
    tm = TM_MERGE
    nt = ya.shape[2] // tm
    tok = pl.BlockSpec((tm, D_MODEL), lambda i: (i, 0))
    y_spec = pl.BlockSpec((1, OUT_GROUPS, tm, LANES), lambda i: (i // nt, 0, i % nt, 0))
    return pl.pallas_call(
        _merge_kernel,
        grid=(n // tm,),
        in_specs=[tok, _const_spec(g1.shape), y_spec, y_spec, y_spec,
                  _const_spec(wg.shape), _const_spec(wb.shape), _const_spec(wo.shape)],
        out_specs=tok,
        out_shape=jax.ShapeDtypeStruct((n, D_MODEL), F32),
        compiler_params=_cparams(1),
        name="gated_merge",
    )(x2, g1, ya, yb, yc, wg, wb, wo)


FF_CHUNK = 1024


def _mlp_kernel(x_ref, g2_ref, w1_ref, w2_ref, gf_ref, o_ref, *, final):
    x = x_ref[...]
    h = _rms(x, g2_ref[...]).astype(BF16)
    acc = x
    for c in range(D_FF // FF_CHUNK):
        cols = pl.ds(c * FF_CHUNK, FF_CHUNK)
        hid = jnp.dot(h, w1_ref[:, cols], preferred_element_type=F32)
        hid = jnp.square(jnp.maximum(hid, 0.0)).astype(BF16)
        acc = acc + jnp.dot(hid, w2_ref[cols, :], preferred_element_type=F32)
    if final:
        acc = _rms(acc, gf_ref[...])
    o_ref[...] = acc


def _mlp_call(x2, g2, w1, w2, gf, final):
    n = x2.shape[0]
    tm = TM_MLP
    tok = pl.BlockSpec((tm, D_MODEL), lambda i: (i, 0))
    return pl.pallas_call(
        functools.partial(_mlp_kernel, final=final),
        grid=(n // tm,),
        in_specs=[tok, _const_spec(g2.shape), _const_spec(w1.shape),
                  _const_spec(w2.shape), _const_spec(gf.shape)],
        out_specs=tok,
        out_shape=jax.ShapeDtypeStruct((n, D_MODEL), F32),
        compiler_params=_cparams(1),
        name="mlp_final" if final else "mlp",
    )(x2, g2, w1, w2, gf)


def _angles(pos, dim, theta):
    inv_freq = theta ** (-jnp.arange(0, dim, 2, dtype=F32) / dim)
    return pos.astype(F32)[:, None] * inv_freq[None, :]


def _rope_tables(seq):
    t = jnp.arange(seq)
    half = A_HEAD_DIM // 2
    row_ang = _angles(t // GRID_W, half, AXIAL_THETA)
    col_ang = _angles(t % GRID_W, half, AXIAL_THETA)
    b_ang = _angles(t, B_ROT_DIM, ROPE_THETA)
    c_ang = _angles(t, C_ROPE_DIM, MLA_THETA)

    def pair(ang):
        c, s = jnp.cos(ang), jnp.sin(ang)
        return jnp.concatenate([c, c], 1), jnp.concatenate([-s, s], 1)

    one = lambda w: jnp.ones((seq, w), F32)
    zero = lambda w: jnp.zeros((seq, w), F32)

    rc, rs = pair(row_ang)
    cc, cs = pair(col_ang)
    cos_a = jnp.concatenate([rc, cc] * 2, 1)
    sin_a = jnp.concatenate([rs, cs] * 2, 1)

    bc, bs = pair(b_ang)
    cos_b = jnp.concatenate([bc, one(B_QK_DIM - B_ROT_DIM)] * 4, 1)
    sin_b = jnp.concatenate([bs, zero(B_QK_DIM - B_ROT_DIM)] * 4, 1)

    mc, ms = pair(c_ang)
    cos_c = jnp.concatenate([one(64), mc, one(32)], 1)
    sin_c = jnp.concatenate([zero(64), ms, zero(32)], 1)
    return cos_a, sin_a, cos_b, sin_b, cos_c, sin_c


def _layer_weights(l, w_in, c_w_uq, c_w_ukv, w_branch):
    w = w_in[l]
    a_q = w[:, 0:512].reshape(D_MODEL, A_HEADS, A_HEAD_DIM)
    a_q = a_q[:, jnp.array(A_HEAD_ORDER), :].reshape(D_MODEL, 512)
    c_kr = jnp.zeros((D_MODEL, LANES), F32).at[:, 64:96].set(w[:, 2944:2976])
    w_main = jnp.concatenate([a_q, w[:, 512:2944], c_kr], axis=1).astype(BF16)
    w_gate = w[:, 2976:].astype(BF16)

    uq = c_w_uq[l].reshape(C_Q_RANK, C_HEADS, C_NOPE_DIM + C_ROPE_DIM)
    uq = jnp.pad(uq, ((0, 0), (0, 0), (0, LANES - C_NOPE_DIM - C_ROPE_DIM)))
    wuq = uq.reshape(C_Q_RANK, C_HEADS * LANES).astype(BF16)
    ukv = c_w_ukv[l].reshape(C_KV_RANK, C_HEADS, C_NOPE_DIM + C_V_DIM)
    kn = jnp.pad(ukv[:, :, :C_NOPE_DIM], ((0, 0), (0, 0), (0, LANES - C_NOPE_DIM)))
    wkn = kn.reshape(C_KV_RANK, C_HEADS * LANES).astype(BF16)
    wv = ukv[:, :, C_NOPE_DIM:].reshape(C_KV_RANK, C_HEADS * C_V_DIM).astype(BF16)

    wb = w_branch[l]
    wb_a = wb[0].reshape(A_HEADS, A_HEAD_DIM, D_MODEL)[jnp.array(A_HEAD_ORDER)]
    wb = jnp.stack([wb_a.reshape(BRANCH_W, D_MODEL), wb[1], wb[2]]).astype(BF16)
    return w_main, w_gate, wuq, wkn, wv, wb


def kernel(x, ln1_g, w_in, a_q_norm, a_k_norm, b_lambda, b_subln, c_q_norm, c_kv_norm,
           c_w_uq, c_w_ukv, w_branch, w_out, ln2_g, w_ff1, w_ff2, final_g):
    b, s, d = x.shape
    n = b * s
    tables = _rope_tables(s)
    gidx = jnp.arange(BRANCH_W) // A_HEAD_DIM
    gmat = (gidx[:, None] == gidx[None, :]).astype(BF16)
    gf = final_g.reshape(1, d)

    x2 = x.reshape(n, d)
    for l in range(DEPTH):
        w_main, w_gate, wuq, wkn, wv, wb = _layer_weights(l, w_in, c_w_uq, c_w_ukv, w_branch)
        g1 = ln1_g[l].reshape(1, d)
        qa, ka, va, qb, kb, vb, qc, kc, vc = _proj_call(
            x2, g1, w_main,
            jnp.tile(a_q_norm[l], A_HEADS).reshape(1, 512),
            jnp.tile(a_k_norm[l], A_KV_HEADS).reshape(1, 128),
            c_q_norm[l].reshape(1, C_Q_RANK), c_kv_norm[l].reshape(1, C_KV_RANK),
            wuq, wkn, wv, gmat, tables, b, s)

        ya = _attn_call(_attn_a_kernel, "attn_gqa", qa, ka, va, (), 2, RC_GQA)
        lambda_init = 0.8 - 0.6 * math.exp(-0.3 * l)
        yb = _attn_call(_attn_b_kernel, "attn_diff", qb, kb, vb,
                        (b_lambda[l], jnp.full((1, LANES), lambda_init, F32),
                         jnp.tile(b_subln[l], 2).reshape(1, LANES)), 4, RC_DIFF)
        yc = _attn_call(_attn_c_kernel, "attn_mla", qc, kc, vc, (), 2, RC_MLA)

        x2 = _merge_call(x2, g1, ya, yb, yc, w_gate, wb, w_out[l].astype(BF16))
        x2 = _mlp_call(x2, ln2_g[l].reshape(1, d), w_ff1[l].astype(BF16),
                       w_ff2[l].astype(BF16), gf, l == DEPTH - 1)
    return x2.reshape(b, s, d)
```

```python
import functools
import math

import jax
import jax.numpy as jnp
from jax import lax
from jax.experimental import pallas as pl
from jax.experimental.pallas import tpu as pltpu

F32 = jnp.float32
BF16 = jnp.bfloat16

D_MODEL = 1024
DEPTH = 4
GRID_W = 64
EPS = 1e-6

A_HEADS = 8
A_KV_HEADS = 2
A_HEAD_DIM = 64
AXIAL_THETA = 10000.0

B_HEADS = 8
B_QK_DIM = 32
B_V_DIM = 64
B_ROT_DIM = 8
ROPE_THETA = 500000.0

C_HEADS = 8
C_Q_RANK = 384
C_KV_RANK = 256
C_NOPE_DIM = 64
C_ROPE_DIM = 32
C_V_DIM = 64
MLA_THETA = 10000.0

N_BRANCH = 3
BRANCH_W = 512
D_FF = 4 * D_MODEL

LANES = 128
LOG2E = math.log2(math.e)
VMEM_LIMIT = 52 * 1024 * 1024
OUT_GROUPS = BRANCH_W // LANES

OFF_CQ, OFF_CKV, OFF_CKR = 0, 384, 640
OFF_AQ, OFF_AK = 768, 1280
OFF_BQ, OFF_BK = 1408, 1920
OFF_AV, OFF_BV = 2432, 2560
PROJ_COLS = 3072

TM_PROJ = 512
SUB_PROJ = 256
TM_MERGE = 512
TM_MLP = 512
RC_GQA = 512
RC_DIFF = 256
RC_MLA = 256

A_HEAD_ORDER = (0, 4, 1, 5, 2, 6, 3, 7)


def _cparams(n_axes):
    return pltpu.CompilerParams(
        dimension_semantics=("parallel",) * n_axes,
        vmem_limit_bytes=VMEM_LIMIT,
    )


def _const_spec(shape):
    nd = len(shape)
    return pl.BlockSpec(shape, lambda *_: (0,) * nd, pipeline_mode=pl.Buffered(1))


def _rms(x, g):
    return x * lax.rsqrt(jnp.mean(x * x, axis=-1, keepdims=True) + EPS) * g


def _lane_ids(shape):
    return lax.broadcasted_iota(jnp.int32, shape, len(shape) - 1)


def _rope_block(x, cos, sin_signed, first, shift):
    up = pltpu.roll(x, LANES - shift, 1)
    dn = pltpu.roll(x, shift, 1)
    return x * cos + jnp.where(first, up, dn) * sin_signed


def _rope(x, cos, sin_signed, first, shift):
    blocks = [
        _rope_block(x[:, i * LANES:(i + 1) * LANES], cos, sin_signed, first, shift)
        for i in range(x.shape[1] // LANES)
    ]
    return blocks[0] if len(blocks) == 1 else jnp.concatenate(blocks, axis=1)


def _group_sum(x, gmat):
    hi = x.astype(BF16)
    lo = (x - hi.astype(F32)).astype(BF16)
    return (jnp.dot(hi, gmat, preferred_element_type=F32)
            + jnp.dot(lo, gmat, preferred_element_type=F32))


def _store_groups(ref, rows, x):
    for g in range(x.shape[1] // LANES):
        ref[0, g, rows, :] = x[:, g * LANES:(g + 1) * LANES].astype(ref.dtype)


def _store_groups_t(ref, rows, x):
    for g in range(x.shape[1] // LANES):
        ref[0, g, :, rows] = x[:, g * LANES:(g + 1) * LANES].T.astype(ref.dtype)


def _store_values(ref, rows, v):
    ones = jnp.ones((v.shape[0], LANES), ref.dtype)
    for g in range(v.shape[1] // LANES):
        ref[0, g, rows, :] = jnp.concatenate(
            [v[:, g * LANES:(g + 1) * LANES].astype(ref.dtype), ones], axis=1)


def _proj_kernel(x_ref, g1_ref, w_ref, aqg_ref, akg_ref, cqg_ref, ckvg_ref,
                 wuq_ref, wkn_ref, wv_ref, gmat_ref,
                 cosa_ref, sina_ref, cosb_ref, sinb_ref, cosc_ref, sinc_ref,
                 qa_ref, ka_ref, va_ref, qb_ref, kb_ref, vb_ref,
                 qc_ref, kc_ref, vc_ref):
    lane = _lane_ids((1, LANES))
    first_a = (lane % 32) < 16
    first_b = (lane % 32) < 4
    first_c = (lane >= 64) & (lane < 80)
    gmat = gmat_ref[...]

    def head_norm(t, g, gm):
        ms = _group_sum(t * t, gm) * (1.0 / A_HEAD_DIM)
        return t * lax.rsqrt(ms + EPS) * g

    for sub in range(x_ref.shape[0] // SUB_PROJ):
        rows = pl.ds(sub * SUB_PROJ, SUB_PROJ)
        h = _rms(x_ref[rows, :], g1_ref[...]).astype(BF16)
        proj = jnp.dot(h, w_ref[...], preferred_element_type=F32)

        def p(off, width):
            return proj[:, off:off + width]

        cosa, sina = cosa_ref[rows, :], sina_ref[rows, :]
        cosb, sinb = cosb_ref[rows, :], sinb_ref[rows, :]
        cosc, sinc = cosc_ref[rows, :], sinc_ref[rows, :]

        cq = _rms(p(OFF_CQ, C_Q_RANK), cqg_ref[...]).astype(BF16)
        q_up = jnp.dot(cq, wuq_ref[...], preferred_element_type=F32)
        _store_groups(qc_ref, rows, _rope(q_up, cosc, sinc, first_c, 16)
                      * ((C_NOPE_DIM + C_ROPE_DIM) ** -0.5 * LOG2E))
        ckv = _rms(p(OFF_CKV, C_KV_RANK), ckvg_ref[...]).astype(BF16)
        k_nope = jnp.dot(ckv, wkn_ref[...], preferred_element_type=F32)
        k_rope = _rope(p(OFF_CKR, LANES), cosc, sinc, first_c, 16)
        _store_groups_t(kc_ref, rows,
                        k_nope + jnp.concatenate([k_rope] * C_HEADS, axis=1))
        _store_values(vc_ref, rows, jnp.dot(ckv, wv_ref[...], preferred_element_type=F32))

        aq = head_norm(p(OFF_AQ, 512), aqg_ref[...], gmat)
        _store_groups(qa_ref, rows, _rope(aq, cosa, sina, first_a, 16)
                      * (A_HEAD_DIM ** -0.5 * LOG2E))
        ak = head_norm(p(OFF_AK, 128), akg_ref[...], gmat[:LANES, :LANES])
        _store_groups_t(ka_ref, rows, _rope(ak, cosa, sina, first_a, 16))

        _store_groups(qb_ref, rows, _rope(p(OFF_BQ, 512), cosb, sinb, first_b, 4)
                      * (B_QK_DIM ** -0.5 * LOG2E))
        _store_groups_t(kb_ref, rows,
                        _rope(p(OFF_BK, 512), cosb, sinb, first_b, 4))

        _store_values(va_ref, rows, p(OFF_AV, 128))
        _store_values(vb_ref, rows, p(OFF_BV, 512))


def _proj_call(x2, g1, w_main, aqg, akg, cqg, ckvg, wuq, wkn, wv, gmat, tables, batch, seq):
    n = x2.shape[0]
    tm = TM_PROJ
    nt = seq // tm
    tab_spec = pl.BlockSpec((tm, LANES), lambda i: (i % nt, 0))

    def rows(g, w):
        return (jax.ShapeDtypeStruct((batch, g, seq, w), BF16),
                pl.BlockSpec((1, g, tm, w), lambda i: (i // nt, 0, i % nt, 0)))

    def cols(g):
        return (jax.ShapeDtypeStruct((batch, g, LANES, seq), BF16),
                pl.BlockSpec((1, g, LANES, tm), lambda i: (i // nt, 0, 0, i % nt)))

    outs = [rows(4, LANES), cols(1), rows(1, 2 * LANES),
            rows(4, LANES), cols(4), rows(4, 2 * LANES),
            rows(8, LANES), cols(8), rows(4, 2 * LANES)]
    return pl.pallas_call(
        _proj_kernel,
        grid=(n // tm,),
        in_specs=[pl.BlockSpec((tm, D_MODEL), lambda i: (i, 0)),
                  _const_spec(g1.shape), _const_spec(w_main.shape),
                  _const_spec(aqg.shape), _const_spec(akg.shape),
                  _const_spec(cqg.shape), _const_spec(ckvg.shape),
                  _const_spec(wuq.shape), _const_spec(wkn.shape), _const_spec(wv.shape),
                  _const_spec(gmat.shape)] + [tab_spec] * 6,
        out_specs=[o[1] for o in outs],
        out_shape=[o[0] for o in outs],
        compiler_params=_cparams(1),
        name="proj_prep",
    )(x2, g1, w_main, aqg, akg, cqg, ckvg, wuq, wkn, wv, gmat, *tables)


def _run_pipeline(n_items, stage1, stage2, stage3):
    assert n_items % 2 == 0
    i32 = jnp.int32
    stage1(i32(0), 0)
    stage1(i32(1), 1)
    stage2(0)

    def body(u, carry):
        t = 2 * u + 1
        stage1(t + 1, 0)
        stage2(1)
        stage3(t - 1, 0)
        stage1(t + 2, 1)
        stage2(0)
        stage3(t, 1)
        return carry

    lax.fori_loop(0, (n_items - 2) // 2, body, 0)
    stage2(1)
    stage3(i32(n_items - 2), 0)
    stage3(i32(n_items - 1), 1)


def _item_locator(n_rows, rows_per_item):
    n_chunks = n_rows // rows_per_item
    assert n_chunks & (n_chunks - 1) == 0
    shift = n_chunks.bit_length() - 1

    def locate(t):
        c = t & (n_chunks - 1)
        return t >> shift, pl.ds(pl.multiple_of(c * rows_per_item, rows_per_item),
                                 rows_per_item)

    return locate, OUT_GROUPS * n_chunks


def _attn_pipeline(rows_per_item, n_sub, get_qk, get_v, finalize, o_ref,
                   s_buf, m_buf, e_buf):
    locate, n_items = _item_locator(o_ref.shape[2], rows_per_item)

    def stage1(t, slot):
        g, rows = locate(t)
        for j in range(n_sub):
            q, kt = get_qk(g, rows, j)
            s = jnp.dot(q, kt, preferred_element_type=F32)
            s_buf[slot, j] = s
            m_buf[slot, j] = jnp.max(s, axis=-1, keepdims=True)

    def stage2(slot):
        for j in range(n_sub):
            e_buf[slot, j] = jnp.exp2(s_buf[slot, j] - m_buf[slot, j]).astype(BF16)

    def stage3(t, slot):
        g, rows = locate(t)
        outs = []
        for j in range(n_sub):
            r = jnp.dot(e_buf[slot, j], get_v(g, j), preferred_element_type=F32)
            outs.append(r[:, :LANES] / r[:, LANES:])
        o_ref[0, g, rows, :] = finalize(outs).astype(o_ref.dtype)

    _run_pipeline(n_items, stage1, stage2, stage3)


def _attn_a_kernel(q_ref, k_ref, v_ref, o_ref, s_buf, m_buf, e_buf, *, rc):
    lane = _lane_ids((1, LANES))
    low = lane < 64

    def get_qk(g, rows, j):
        q = q_ref[0, g, rows, :]
        zero = jnp.zeros_like(q)
        return (jnp.where(low, q, zero) if j == 0 else jnp.where(low, zero, q)), k_ref[0, 0]

    def get_v(g, j):
        return v_ref[0, 0]

    def finalize(outs):
        return jnp.where(low, outs[0], outs[1])

    _attn_pipeline(rc, 2, get_qk, get_v, finalize, o_ref, s_buf, m_buf, e_buf)


def _attn_b_kernel(q_ref, k_ref, v_ref, lam_ref, linit_ref, subg_ref, o_ref,
                   s_buf, m_buf, e_buf, *, rc):
    lane = _lane_ids((1, LANES))
    lf = lam_ref[...]
    lam = (jnp.exp(jnp.sum(lf[0:1] * lf[1:2], axis=-1, keepdims=True))
           - jnp.exp(jnp.sum(lf[2:3] * lf[3:4], axis=-1, keepdims=True))
           + linit_ref[:, 0:1])
    post = subg_ref[...] * (1.0 - linit_ref[...])
    low = lane < 64

    def get_qk(g, rows, j):
        q = q_ref[0, g, rows, :]
        sel = (lane >= j * B_QK_DIM) & (lane < (j + 1) * B_QK_DIM)
        return jnp.where(sel, q, jnp.zeros_like(q)), k_ref[0, g]

    def get_v(g, j):
        return v_ref[0, g]

    def finalize(outs):
        heads = []
        for hh in range(2):
            o = outs[2 * hh] - lam * outs[2 * hh + 1]
            in_head = low if hh == 0 else jnp.logical_not(low)
            ms = jnp.sum(jnp.where(in_head, o * o, 0.0), axis=-1,
                         keepdims=True) * (1.0 / B_V_DIM)
            heads.append(o * lax.rsqrt(ms + EPS))
        return jnp.where(low, heads[0], heads[1]) * post

    _attn_pipeline(rc, 4, get_qk, get_v, finalize, o_ref, s_buf, m_buf, e_buf)


def _attn_c_kernel(q_ref, k_ref, v_ref, o_ref, s_buf, m_buf, e_buf, *, rc):
    lane = _lane_ids((1, LANES))
    low = lane < 64

    def get_qk(g, rows, j):
        return q_ref[0, 2 * g + j, rows, :], k_ref[0, 2 * g + j]

    def get_v(g, j):
        return v_ref[0, g]

    def finalize(outs):
        return jnp.where(low, outs[0], outs[1])

    _attn_pipeline(rc, 2, get_qk, get_v, finalize, o_ref, s_buf, m_buf, e_buf)


def _softmax_scratch(n_sub, rc, s):
    return [pltpu.VMEM((2, n_sub, rc, s), F32),
            pltpu.VMEM((2, n_sub, rc, 1), F32),
            pltpu.VMEM((2, n_sub, rc, s), BF16)]


def _attn_call(kernel, name, q, kt, v, extra, n_sub, rc):
    b, _, s, _ = q.shape
    rc = min(rc, s)

    def batch_spec(a):
        return pl.BlockSpec((1,) + a.shape[1:], lambda bi: (bi, 0, 0, 0))

    return pl.pallas_call(
        functools.partial(kernel, rc=rc),
        grid=(b,),
        in_specs=[batch_spec(q), batch_spec(kt), batch_spec(v)]
        + [_const_spec(e.shape) for e in extra],
        out_specs=pl.BlockSpec((1, OUT_GROUPS, s, LANES), lambda bi: (bi, 0, 0, 0)),
        out_shape=jax.ShapeDtypeStruct((b, OUT_GROUPS, s, LANES), BF16),
        scratch_shapes=_softmax_scratch(n_sub, rc, s),
        compiler_params=_cparams(1),
        name=name,
    )(q, kt, v, *extra)


def _merge_kernel(x_ref, g1_ref, ya_ref, yb_ref, yc_ref, wg_ref, wb_ref, wo_ref, o_ref):
    x = x_ref[...]
    h = _rms(x, g1_ref[...]).astype(BF16)
    merged = None
    for n, y_ref in enumerate((ya_ref, yb_ref, yc_ref)):
        logits = jnp.dot(h, wg_ref[:, n * D_MODEL:(n + 1) * D_MODEL],
                         preferred_element_type=F32)
        y = jnp.concatenate([y_ref[0, g] for g in range(OUT_GROUPS)], axis=1)
        z = jnp.dot(y, wb_ref[n], preferred_element_type=F32)
        term = jax.nn.sigmoid(logits) * z
        merged = term if merged is None else merged + term
    o_ref[...] = x + jnp.dot(merged.astype(BF16), wo_ref[...],
                             preferred_element_type=F32)


def _merge_call(x2, g1, ya, yb, yc, wg, wb, wo):
    n = x2.shape[0]
    tm = TM_MERGE
    nt = ya.shape[2] // tm
    tok = pl.BlockSpec((tm, D_MODEL), lambda i: (i, 0))
    y_spec = pl.BlockSpec((1, OUT_GROUPS, tm, LANES), lambda i: (i // nt, 0, i % nt, 0))
    return pl.pallas_call(
        _merge_kernel,
        grid=(n // tm,),
        in_specs=[tok, _const_spec(g1.shape), y_spec, y_spec, y_spec,
                  _const_spec(wg.shape), _const_spec(wb.shape), _const_spec(wo.shape)],
        out_specs=tok,
        out_shape=jax.ShapeDtypeStruct((n, D_MODEL), F32),
        compiler_params=_cparams(1),
        name="gated_merge",
    )(x2, g1, ya, yb, yc, wg, wb, wo)


FF_CHUNK = 1024


def _mlp_kernel(x_ref, g2_ref, w1_ref, w2_ref, gf_ref, o_ref, *, final):
    x = x_ref[...]
    h = _rms(x, g2_ref[...]).astype(BF16)
    acc = x
    for c in range(D_FF // FF_CHUNK):
        cols = pl.ds(c * FF_CHUNK, FF_CHUNK)
        hid = jnp.dot(h, w1_ref[:, cols], preferred_element_type=F32)
        hid = jnp.square(jnp.maximum(hid, 0.0)).astype(BF16)
        acc = acc + jnp.dot(hid, w2_ref[cols, :], preferred_element_type=F32)
    if final:
        acc = _rms(acc, gf_ref[...])
    o_ref[...] = acc


def _mlp_call(x2, g2, w1, w2, gf, final):
    n = x2.shape[0]
    tm = TM_MLP
    tok = pl.BlockSpec((tm, D_MODEL), lambda i: (i, 0))
    return pl.pallas_call(
        functools.partial(_mlp_kernel, final=final),
        grid=(n // tm,),
        in_specs=[tok, _const_spec(g2.shape), _const_spec(w1.shape),
                  _const_spec(w2.shape), _const_spec(gf.shape)],
        out_specs=tok,
        out_shape=jax.ShapeDtypeStruct((n, D_MODEL), F32),
        compiler_params=_cparams(1),
        name="mlp_final" if final else "mlp",
    )(x2, g2, w1, w2, gf)


def _angles(pos, dim, theta):
    inv_freq = theta ** (-jnp.arange(0, dim, 2, dtype=F32) / dim)
    return pos.astype(F32)[:, None] * inv_freq[None, :]


def _rope_tables(seq):
    t = jnp.arange(seq)
    half = A_HEAD_DIM // 2
    row_ang = _angles(t // GRID_W, half, AXIAL_THETA)
    col_ang = _angles(t % GRID_W, half, AXIAL_THETA)
    b_ang = _angles(t, B_ROT_DIM, ROPE_THETA)
    c_ang = _angles(t, C_ROPE_DIM, MLA_THETA)

    def pair(ang):
        c, s = jnp.cos(ang), jnp.sin(ang)
        return jnp.concatenate([c, c], 1), jnp.concatenate([-s, s], 1)

    one = lambda w: jnp.ones((seq, w), F32)
    zero = lambda w: jnp.zeros((seq, w), F32)

    rc, rs = pair(row_ang)
    cc, cs = pair(col_ang)
    cos_a = jnp.concatenate([rc, cc] * 2, 1)
    sin_a = jnp.concatenate([rs, cs] * 2, 1)

    bc, bs = pair(b_ang)
    cos_b = jnp.concatenate([bc, one(B_QK_DIM - B_ROT_DIM)] * 4, 1)
    sin_b = jnp.concatenate([bs, zero(B_QK_DIM - B_ROT_DIM)] * 4, 1)

    mc, ms = pair(c_ang)
    cos_c = jnp.concatenate([one(64), mc, one(32)], 1)
    sin_c = jnp.concatenate([zero(64), ms, zero(32)], 1)
    return cos_a, sin_a, cos_b, sin_b, cos_c, sin_c


def _layer_weights(l, w_in, c_w_uq, c_w_ukv, w_branch):
    w = w_in[l]
    a_q = w[:, 0:512].reshape(D_MODEL, A_HEADS, A_HEAD_DIM)
    a_q = a_q[:, jnp.array(A_HEAD_ORDER), :].reshape(D_MODEL, 512)
    c_kr = jnp.zeros((D_MODEL, LANES), w.dtype).at[:, 64:96].set(w[:, 2944:2976])
    w_main = jnp.concatenate(
        [w[:, 2304:2944], c_kr, a_q, w[:, 512:640], w[:, 768:1792],
         w[:, 640:768], w[:, 1792:2304]], axis=1)
    assert w_main.shape[1] == PROJ_COLS
    w_gate = w[:, 2976:]

    uq = c_w_uq[l].reshape(C_Q_RANK, C_HEADS, C_NOPE_DIM + C_ROPE_DIM)
    uq = jnp.pad(uq, ((0, 0), (0, 0), (0, LANES - C_NOPE_DIM - C_ROPE_DIM)))
    wuq = uq.reshape(C_Q_RANK, C_HEADS * LANES)
    ukv = c_w_ukv[l].reshape(C_KV_RANK, C_HEADS, C_NOPE_DIM + C_V_DIM)
    kn = jnp.pad(ukv[:, :, :C_NOPE_DIM], ((0, 0), (0, 0), (0, LANES - C_NOPE_DIM)))
    wkn = kn.reshape(C_KV_RANK, C_HEADS * LANES)
    wv = ukv[:, :, C_NOPE_DIM:].reshape(C_KV_RANK, C_HEADS * C_V_DIM)

    wb = w_branch[l]
    wb_a = wb[0].reshape(A_HEADS, A_HEAD_DIM, D_MODEL)[jnp.array(A_HEAD_ORDER)]
    wb = jnp.stack([wb_a.reshape(BRANCH_W, D_MODEL), wb[1], wb[2]])
    return w_main, w_gate, wuq, wkn, wv, wb


def kernel(x, ln1_g, w_in, a_q_norm, a_k_norm, b_lambda, b_subln, c_q_norm, c_kv_norm,
           c_w_uq, c_w_ukv, w_branch, w_out, ln2_g, w_ff1, w_ff2, final_g):
    b, s, d = x.shape
    n = b * s
    tables = _rope_tables(s)
    gidx = jnp.arange(BRANCH_W) // A_HEAD_DIM
    gmat = (gidx[:, None] == gidx[None, :]).astype(BF16)
    gf = final_g.reshape(1, d)
    w_in, c_w_uq, c_w_ukv, w_branch, w_out, w_ff1, w_ff2 = (
        t.astype(BF16) for t in (w_in, c_w_uq, c_w_ukv, w_branch, w_out, w_ff1, w_ff2))

    x2 = x.reshape(n, d)
    for l in range(DEPTH):
        w_main, w_gate, wuq, wkn, wv, wb = _layer_weights(l, w_in, c_w_uq, c_w_ukv, w_branch)
        g1 = ln1_g[l].reshape(1, d)
        qa, ka, va, qb, kb, vb, qc, kc, vc = _proj_call(
            x2, g1, w_main,
            jnp.tile(a_q_norm[l], A_HEADS).reshape(1, 512),
            jnp.tile(a_k_norm[l], A_KV_HEADS).reshape(1, 128),
            c_q_norm[l].reshape(1, C_Q_RANK), c_kv_norm[l].reshape(1, C_KV_RANK),
            wuq, wkn, wv, gmat, tables, b, s)

        ya = _attn_call(_attn_a_kernel, "attn_gqa", qa, ka, va, (), 2, RC_GQA)
        lambda_init = 0.8 - 0.6 * math.exp(-0.3 * l)
        yb = _attn_call(_attn_b_kernel, "attn_diff", qb, kb, vb,
                        (b_lambda[l], jnp.full((1, LANES), lambda_init, F32),
                         jnp.tile(b_subln[l], 2).reshape(1, LANES)), 4, RC_DIFF)
        yc = _attn_call(_attn_c_kernel, "attn_mla", qc, kc, vc, (), 2, RC_MLA)

        x2 = _merge_call(x2, g1, ya, yb, yc, w_gate, wb, w_out[l])
        x2 = _mlp_call(x2, ln2_g[l].reshape(1, d), w_ff1[l], w_ff2[l], gf, l == DEPTH - 1)
    return x2.reshape(b, s, d)
```

```python
import functools
import math

import jax
import jax.numpy as jnp
from jax import lax
from jax.experimental import pallas as pl
from jax.experimental.pallas import tpu as pltpu

F32 = jnp.float32
BF16 = jnp.bfloat16

D_MODEL = 1024
DEPTH = 4
GRID_W = 64
EPS = 1e-6

A_HEADS = 8
A_KV_HEADS = 2
A_HEAD_DIM = 64
AXIAL_THETA = 10000.0

B_HEADS = 8
B_QK_DIM = 32
B_V_DIM = 64
B_ROT_DIM = 8
ROPE_THETA = 500000.0

C_HEADS = 8
C_Q_RANK = 384
C_KV_RANK = 256
C_NOPE_DIM = 64
C_ROPE_DIM = 32
C_V_DIM = 64
MLA_THETA = 10000.0

N_BRANCH = 3
BRANCH_W = 512
D_FF = 4 * D_MODEL

LANES = 128
LOG2E = math.log2(math.e)
VMEM_LIMIT = 52 * 1024 * 1024
OUT_GROUPS = BRANCH_W // LANES

OFF_CQ, OFF_CKV, OFF_CKR = 0, 384, 640
OFF_AQ, OFF_AK = 768, 1280
OFF_BQ, OFF_BK = 1408, 1920
OFF_AV, OFF_BV = 2432, 2560
PROJ_COLS = 3072

TM_PROJ = 512
SUB_PROJ = 256
TM_MERGE = 512
TM_MLP = 512
RC_GQA = 512
RC_DIFF = 512
RC_MLA = 256

A_HEAD_ORDER = (0, 4, 1, 5, 2, 6, 3, 7)


def _cparams(n_axes):
    return pltpu.CompilerParams(
        dimension_semantics=("parallel",) * n_axes,
        vmem_limit_bytes=VMEM_LIMIT,
    )


def _const_spec(shape):
    nd = len(shape)
    return pl.BlockSpec(shape, lambda *_: (0,) * nd, pipeline_mode=pl.Buffered(1))


def _rms(x, g):
    return x * lax.rsqrt(jnp.mean(x * x, axis=-1, keepdims=True) + EPS) * g


def _lane_ids(shape):
    return lax.broadcasted_iota(jnp.int32, shape, len(shape) - 1)


def _rope_block(x, cos, sin_signed, first, shift):
    up = pltpu.roll(x, LANES - shift, 1)
    dn = pltpu.roll(x, shift, 1)
    return x * cos + jnp.where(first, up, dn) * sin_signed


def _rope(x, cos, sin_signed, first, shift):
    blocks = [
        _rope_block(x[:, i * LANES:(i + 1) * LANES], cos, sin_signed, first, shift)
        for i in range(x.shape[1] // LANES)
    ]
    return blocks[0] if len(blocks) == 1 else jnp.concatenate(blocks, axis=1)


def _group_sum(x, gmat):
    hi = x.astype(BF16)
    lo = (x - hi.astype(F32)).astype(BF16)
    return (jnp.dot(hi, gmat, preferred_element_type=F32)
            + jnp.dot(lo, gmat, preferred_element_type=F32))


def _store_groups(ref, rows, x):
    for g in range(x.shape[1] // LANES):
        ref[0, g, rows, :] = x[:, g * LANES:(g + 1) * LANES].astype(ref.dtype)


def _store_groups_t(ref, rows, x):
    for g in range(x.shape[1] // LANES):
        ref[0, g, :, rows] = x[:, g * LANES:(g + 1) * LANES].T.astype(ref.dtype)


def _store_values(ref, rows, v):
    ones = jnp.ones((v.shape[0], LANES), ref.dtype)
    for g in range(v.shape[1] // LANES):
        ref[0, g, rows, :] = jnp.concatenate(
            [v[:, g * LANES:(g + 1) * LANES].astype(ref.dtype), ones], axis=1)


def _proj_kernel(x_ref, g1_ref, w_ref, aqg_ref, akg_ref, cqg_ref, ckvg_ref,
                 wuq_ref, wkn_ref, wv_ref, gmat_ref,
                 cosa_ref, sina_ref, cosb_ref, sinb_ref, cosc_ref, sinc_ref,
                 qa_ref, ka_ref, va_ref, qb_ref, kb_ref, vb_ref,
                 qc_ref, kc_ref, vc_ref):
    lane = _lane_ids((1, LANES))
    first_a = (lane % 32) < 16
    first_b = (lane % 32) < 4
    first_c = (lane >= 64) & (lane < 80)
    gmat = gmat_ref[...]

    def head_norm(t, g, gm):
        ms = _group_sum(t * t, gm) * (1.0 / A_HEAD_DIM)
        return t * lax.rsqrt(ms + EPS) * g

    for sub in range(x_ref.shape[0] // SUB_PROJ):
        rows = pl.ds(sub * SUB_PROJ, SUB_PROJ)
        h = _rms(x_ref[rows, :], g1_ref[...]).astype(BF16)
        proj = jnp.dot(h, w_ref[...], preferred_element_type=F32)

        def p(off, width):
            return proj[:, off:off + width]

        cosa, sina = cosa_ref[rows, :], sina_ref[rows, :]
        cosb, sinb = cosb_ref[rows, :], sinb_ref[rows, :]
        cosc, sinc = cosc_ref[rows, :], sinc_ref[rows, :]

        cq = _rms(p(OFF_CQ, C_Q_RANK), cqg_ref[...]).astype(BF16)
        q_up = jnp.dot(cq, wuq_ref[...], preferred_element_type=F32)
        _store_groups(qc_ref, rows, _rope(q_up, cosc, sinc, first_c, 16)
                      * ((C_NOPE_DIM + C_ROPE_DIM) ** -0.5 * LOG2E))
        ckv = _rms(p(OFF_CKV, C_KV_RANK), ckvg_ref[...]).astype(BF16)
        k_nope = jnp.dot(ckv, wkn_ref[...], preferred_element_type=F32)
        k_rope = _rope(p(OFF_CKR, LANES), cosc, sinc, first_c, 16)
        _store_groups_t(kc_ref, rows,
                        k_nope + jnp.concatenate([k_rope] * C_HEADS, axis=1))
        _store_values(vc_ref, rows, jnp.dot(ckv, wv_ref[...], preferred_element_type=F32))

        aq = head_norm(p(OFF_AQ, 512), aqg_ref[...], gmat)
        _store_groups(qa_ref, rows, _rope(aq, cosa, sina, first_a, 16)
                      * (A_HEAD_DIM ** -0.5 * LOG2E))
        ak = head_norm(p(OFF_AK, 128), akg_ref[...], gmat[:LANES, :LANES])
        _store_groups_t(ka_ref, rows, _rope(ak, cosa, sina, first_a, 16))

        _store_groups(qb_ref, rows, _rope(p(OFF_BQ, 512), cosb, sinb, first_b, 4)
                      * (B_QK_DIM ** -0.5 * LOG2E))
        _store_groups_t(kb_ref, rows,
                        _rope(p(OFF_BK, 512), cosb, sinb, first_b, 4))

        _store_values(va_ref, rows, p(OFF_AV, 128))
        _store_values(vb_ref, rows, p(OFF_BV, 512))


def _proj_call(x2, g1, w_main, aqg, akg, cqg, ckvg, wuq, wkn, wv, gmat, tables, batch, seq):
    n = x2.shape[0]
    tm = TM_PROJ
    nt = seq // tm
    tab_spec = pl.BlockSpec((tm, LANES), lambda i: (i % nt, 0))

    def rows(g, w):
        return (jax.ShapeDtypeStruct((batch, g, seq, w), BF16),
                pl.BlockSpec((1, g, tm, w), lambda i: (i // nt, 0, i % nt, 0)))

    def cols(g):
        return (jax.ShapeDtypeStruct((batch, g, LANES, seq), BF16),
                pl.BlockSpec((1, g, LANES, tm), lambda i: (i // nt, 0, 0, i % nt)))

    outs = [rows(4, LANES), cols(1), rows(1, 2 * LANES),
            rows(4, LANES), cols(4), rows(4, 2 * LANES),
            rows(8, LANES), cols(8), rows(4, 2 * LANES)]
    return pl.pallas_call(
        _proj_kernel,
        grid=(n // tm,),
        in_specs=[pl.BlockSpec((tm, D_MODEL), lambda i: (i, 0)),
                  _const_spec(g1.shape), _const_spec(w_main.shape),
                  _const_spec(aqg.shape), _const_spec(akg.shape),
                  _const_spec(cqg.shape), _const_spec(ckvg.shape),
                  _const_spec(wuq.shape), _const_spec(wkn.shape), _const_spec(wv.shape),
                  _const_spec(gmat.shape)] + [tab_spec] * 6,
        out_specs=[o[1] for o in outs],
        out_shape=[o[0] for o in outs],
        compiler_params=_cparams(1),
        name="proj_prep",
    )(x2, g1, w_main, aqg, akg, cqg, ckvg, wuq, wkn, wv, gmat, *tables)


def _run_pipeline(n_items, stage1, stage2, stage3):
    assert n_items % 2 == 0
    i32 = jnp.int32
    stage1(i32(0), 0)
    stage1(i32(1), 1)
    stage2(0)

    def body(u, carry):
        t = 2 * u + 1
        stage1(t + 1, 0)
        stage2(1)
        stage1(t + 2, 1)
        stage3(t - 1, 0)
        stage2(0)
        stage3(t, 1)
        return carry

    lax.fori_loop(0, (n_items - 2) // 2, body, 0)
    stage2(1)
    stage3(i32(n_items - 2), 0)
    stage3(i32(n_items - 1), 1)


def _item_locator(n_rows, rows_per_item, halves):
    n_chunks = n_rows // rows_per_item
    assert n_chunks & (n_chunks - 1) == 0 and halves in (1, 2)
    shift = n_chunks.bit_length() - 1

    def locate(t):
        u = t >> (halves - 1)
        c = u & (n_chunks - 1)
        return u >> shift, pl.ds(pl.multiple_of(c * rows_per_item, rows_per_item),
                                 rows_per_item)

    return locate, halves * OUT_GROUPS * n_chunks


def _attn_pipeline(rows_per_item, n_sub, halves, get_qk, get_v, finalize, o_ref,
                   s_buf, m_buf, e_buf):
    locate, n_items = _item_locator(o_ref.shape[2], rows_per_item, halves)

    def stage1(t, slot):
        g, rows = locate(t)
        for j in range(n_sub):
            q, kt = get_qk(g, rows, j, slot)
            s = jnp.dot(q, kt, preferred_element_type=F32)
            s_buf[slot, j] = s
            m_buf[slot, j] = jnp.max(s, axis=-1, keepdims=True)

    def stage2(slot):
        for j in range(n_sub):
            e_buf[slot, j] = jnp.exp2(s_buf[slot, j] - m_buf[slot, j]).astype(BF16)

    def stage3(t, slot):
        g, rows = locate(t)
        outs = []
        for j in range(n_sub):
            r = jnp.dot(e_buf[slot, j], get_v(g, j), preferred_element_type=F32)
            outs.append(r[:, :LANES] / r[:, LANES:])
        o = finalize(outs, slot).astype(o_ref.dtype)
        if halves == 1:
            o_ref[0, g, rows, :] = o
        else:
            lo = slot * (LANES // 2)
            o_ref[0, g, rows, lo:lo + LANES // 2] = o[:, lo:lo + LANES // 2]

    _run_pipeline(n_items, stage1, stage2, stage3)


def _attn_a_kernel(q_ref, k_ref, v_ref, o_ref, s_buf, m_buf, e_buf, *, rc):
    lane = _lane_ids((1, LANES))
    low = lane < 64

    def get_qk(g, rows, j, half):
        q = q_ref[0, g, rows, :]
        zero = jnp.zeros_like(q)
        return (jnp.where(low, q, zero) if j == 0 else jnp.where(low, zero, q)), k_ref[0, 0]

    def get_v(g, j):
        return v_ref[0, 0]

    def finalize(outs, half):
        return jnp.where(low, outs[0], outs[1])

    _attn_pipeline(rc, 2, 1, get_qk, get_v, finalize, o_ref, s_buf, m_buf, e_buf)


def _attn_b_kernel(q_ref, k_ref, v_ref, lam_ref, linit_ref, subg_ref, o_ref,
                   s_buf, m_buf, e_buf, *, rc):
    lane = _lane_ids((1, LANES))
    lf = lam_ref[...]
    lam = (jnp.exp(jnp.sum(lf[0:1] * lf[1:2], axis=-1, keepdims=True))
           - jnp.exp(jnp.sum(lf[2:3] * lf[3:4], axis=-1, keepdims=True))
           + linit_ref[:, 0:1])
    post = subg_ref[...] * (1.0 - linit_ref[...])

    def get_qk(g, rows, j, half):
        q = q_ref[0, g, rows, :]
        lo = half * B_V_DIM + j * B_QK_DIM
        sel = (lane >= lo) & (lane < lo + B_QK_DIM)
        return jnp.where(sel, q, jnp.zeros_like(q)), k_ref[0, g]

    def get_v(g, j):
        return v_ref[0, g]

    def finalize(outs, half):
        o = outs[0] - lam * outs[1]
        in_head = (lane >= half * B_V_DIM) & (lane < (half + 1) * B_V_DIM)
        ms = jnp.sum(jnp.where(in_head, o * o, 0.0), axis=-1,
                     keepdims=True) * (1.0 / B_V_DIM)
        return o * lax.rsqrt(ms + EPS) * post

    _attn_pipeline(rc, 2, 2, get_qk, get_v, finalize, o_ref, s_buf, m_buf, e_buf)


def _attn_c_kernel(q_ref, k_ref, v_ref, o_ref, s_buf, m_buf, e_buf, *, rc):
    lane = _lane_ids((1, LANES))
    low = lane < 64

    def get_qk(g, rows, j, half):
        return q_ref[0, 2 * g + j, rows, :], k_ref[0, 2 * g + j]

    def get_v(g, j):
        return v_ref[0, g]

    def finalize(outs, half):
        return jnp.where(low, outs[0], outs[1])

    _attn_pipeline(rc, 2, 1, get_qk, get_v, finalize, o_ref, s_buf, m_buf, e_buf)


def _softmax_scratch(n_sub, rc, s):
    return [pltpu.VMEM((2, n_sub, rc, s), F32),
            pltpu.VMEM((2, n_sub, rc, 1), F32),
            pltpu.VMEM((2, n_sub, rc, s), BF16)]


def _attn_call(kernel, name, q, kt, v, extra, n_sub, rc):
    b, _, s, _ = q.shape
    rc = min(rc, s)

    def batch_spec(a):
        return pl.BlockSpec((1,) + a.shape[1:], lambda bi: (bi, 0, 0, 0))

    return pl.pallas_call(
        functools.partial(kernel, rc=rc),
        grid=(b,),
        in_specs=[batch_spec(q), batch_spec(kt), batch_spec(v)]
        + [_const_spec(e.shape) for e in extra],
        out_specs=pl.BlockSpec((1, OUT_GROUPS, s, LANES), lambda bi: (bi, 0, 0, 0)),
        out_shape=jax.ShapeDtypeStruct((b, OUT_GROUPS, s, LANES), BF16),
        scratch_shapes=_softmax_scratch(n_sub, rc, s),
        compiler_params=_cparams(1),
        name=name,
    )(q, kt, v, *extra)


def _merge_kernel(x_ref, g1_ref, ya_ref, yb_ref, yc_ref, wg_ref, wb_ref, wo_ref, o_ref):
    x = x_ref[...]
    h = _rms(x, g1_ref[...]).astype(BF16)
    merged = None
    for n, y_ref in enumerate((ya_ref, yb_ref, yc_ref)):
        logits = jnp.dot(h, wg_ref[:, n * D_MODEL:(n + 1) * D_MODEL],
                         preferred_element_type=F32)
        y = jnp.concatenate([y_ref[0, g] for g in range(OUT_GROUPS)], axis=1)
        z = jnp.dot(y, wb_ref[n], preferred_element_type=F32)
        term = jax.nn.sigmoid(logits) * z
        merged = term if merged is None else merged + term
    o_ref[...] = x + jnp.dot(merged.astype(BF16), wo_ref[...],
                             preferred_element_type=F32)


def _merge_call(x2, g1, ya, yb, yc, wg, wb, wo):
    n = x2.shape[0]
    tm = TM_MERGE
    nt = ya.shape[2] // tm
    tok = pl.BlockSpec((tm, D_MODEL), lambda i: (i, 0))
    y_spec = pl.BlockSpec((1, OUT_GROUPS, tm, LANES), lambda i: (i // nt, 0, i % nt, 0))
    return pl.pallas_call(
        _merge_kernel,
        grid=(n // tm,),
        in_specs=[tok, _const_spec(g1.shape), y_spec, y_spec, y_spec,
                  _const_spec(wg.shape), _const_spec(wb.shape), _const_spec(wo.shape)],
        out_specs=tok,
        out_shape=jax.ShapeDtypeStruct((n, D_MODEL), F32),
        compiler_params=_cparams(1),
        name="gated_merge",
    )(x2, g1, ya, yb, yc, wg, wb, wo)


FF_CHUNK = 1024


def _mlp_kernel(x_ref, g2_ref, w1_ref, w2_ref, gf_ref, o_ref, *, final):
    x = x_ref[...]
    h = _rms(x, g2_ref[...]).astype(BF16)
    acc = x
    for c in range(D_FF // FF_CHUNK):
        cols = pl.ds(c * FF_CHUNK, FF_CHUNK)
        hid = jnp.dot(h, w1_ref[:, cols], preferred_element_type=F32)
        hid = jnp.square(jnp.maximum(hid, 0.0)).astype(BF16)
        acc = acc + jnp.dot(hid, w2_ref[cols, :], preferred_element_type=F32)
    if final:
        acc = _rms(acc, gf_ref[...])
    o_ref[...] = acc


def _mlp_call(x2, g2, w1, w2, gf, final):
    n = x2.shape[0]
    tm = TM_MLP
    tok = pl.BlockSpec((tm, D_MODEL), lambda i: (i, 0))
    return pl.pallas_call(
        functools.partial(_mlp_kernel, final=final),
        grid=(n // tm,),
        in_specs=[tok, _const_spec(g2.shape), _const_spec(w1.shape),
                  _const_spec(w2.shape), _const_spec(gf.shape)],
        out_specs=tok,
        out_shape=jax.ShapeDtypeStruct((n, D_MODEL), F32),
        compiler_params=_cparams(1),
        name="mlp_final" if final else "mlp",
    )(x2, g2, w1, w2, gf)


def _angles(pos, dim, theta):
    inv_freq = theta ** (-jnp.arange(0, dim, 2, dtype=F32) / dim)
    return pos.astype(F32)[:, None] * inv_freq[None, :]


def _rope_tables(seq):
    t = jnp.arange(seq)
    half = A_HEAD_DIM // 2
    row_ang = _angles(t // GRID_W, half, AXIAL_THETA)
    col_ang = _angles(t % GRID_W, half, AXIAL_THETA)
    b_ang = _angles(t, B_ROT_DIM, ROPE_THETA)
    c_ang = _angles(t, C_ROPE_DIM, MLA_THETA)

    def pair(ang):
        c, s = jnp.cos(ang), jnp.sin(ang)
        return jnp.concatenate([c, c], 1), jnp.concatenate([-s, s], 1)

    one = lambda w: jnp.ones((seq, w), F32)
    zero = lambda w: jnp.zeros((seq, w), F32)

    rc, rs = pair(row_ang)
    cc, cs = pair(col_ang)
    cos_a = jnp.concatenate([rc, cc] * 2, 1)
    sin_a = jnp.concatenate([rs, cs] * 2, 1)

    bc, bs = pair(b_ang)
    cos_b = jnp.concatenate([bc, one(B_QK_DIM - B_ROT_DIM)] * 4, 1)
    sin_b = jnp.concatenate([bs, zero(B_QK_DIM - B_ROT_DIM)] * 4, 1)

    mc, ms = pair(c_ang)
    cos_c = jnp.concatenate([one(64), mc, one(32)], 1)
    sin_c = jnp.concatenate([zero(64), ms, zero(32)], 1)
    return cos_a, sin_a, cos_b, sin_b, cos_c, sin_c


def _layer_weights(l, w_in, c_w_uq, c_w_ukv, w_branch):
    w = w_in[l]
    a_q = w[:, 0:512].reshape(D_MODEL, A_HEADS, A_HEAD_DIM)
    a_q = a_q[:, jnp.array(A_HEAD_ORDER), :].reshape(D_MODEL, 512)
    c_kr = jnp.zeros((D_MODEL, LANES), w.dtype).at[:, 64:96].set(w[:, 2944:2976])
    w_main = jnp.concatenate(
        [w[:, 2304:2944], c_kr, a_q, w[:, 512:640], w[:, 768:1792],
         w[:, 640:768], w[:, 1792:2304]], axis=1)
    assert w_main.shape[1] == PROJ_COLS
    w_gate = w[:, 2976:]

    uq = c_w_uq[l].reshape(C_Q_RANK, C_HEADS, C_NOPE_DIM + C_ROPE_DIM)
    uq = jnp.pad(uq, ((0, 0), (0, 0), (0, LANES - C_NOPE_DIM - C_ROPE_DIM)))
    wuq = uq.reshape(C_Q_RANK, C_HEADS * LANES)
    ukv = c_w_ukv[l].reshape(C_KV_RANK, C_HEADS, C_NOPE_DIM + C_V_DIM)
    kn = jnp.pad(ukv[:, :, :C_NOPE_DIM], ((0, 0), (0, 0), (0, LANES - C_NOPE_DIM)))
    wkn = kn.reshape(C_KV_RANK, C_HEADS * LANES)
    wv = ukv[:, :, C_NOPE_DIM:].reshape(C_KV_RANK, C_HEADS * C_V_DIM)

    wb = w_branch[l]
    wb_a = wb[0].reshape(A_HEADS, A_HEAD_DIM, D_MODEL)[jnp.array(A_HEAD_ORDER)]
    wb = jnp.stack([wb_a.reshape(BRANCH_W, D_MODEL), wb[1], wb[2]])
    return w_main, w_gate, wuq, wkn, wv, wb


def kernel(x, ln1_g, w_in, a_q_norm, a_k_norm, b_lambda, b_subln, c_q_norm, c_kv_norm,
           c_w_uq, c_w_ukv, w_branch, w_out, ln2_g, w_ff1, w_ff2, final_g):
    b, s, d = x.shape
    n = b * s
    tables = _rope_tables(s)
    gidx = jnp.arange(BRANCH_W) // A_HEAD_DIM
    gmat = (gidx[:, None] == gidx[None, :]).astype(BF16)
    gf = final_g.reshape(1, d)
    w_in, c_w_uq, c_w_ukv, w_branch, w_out, w_ff1, w_ff2 = (
        t.astype(BF16) for t in (w_in, c_w_uq, c_w_ukv, w_branch, w_out, w_ff1, w_ff2))

    x2 = x.reshape(n, d)
    for l in range(DEPTH):
        w_main, w_gate, wuq, wkn, wv, wb = _layer_weights(l, w_in, c_w_uq, c_w_ukv, w_branch)
        g1 = ln1_g[l].reshape(1, d)
        qa, ka, va, qb, kb, vb, qc, kc, vc = _proj_call(
            x2, g1, w_main,
            jnp.tile(a_q_norm[l], A_HEADS).reshape(1, 512),
            jnp.tile(a_k_norm[l], A_KV_HEADS).reshape(1, 128),
            c_q_norm[l].reshape(1, C_Q_RANK), c_kv_norm[l].reshape(1, C_KV_RANK),
            wuq, wkn, wv, gmat, tables, b, s)

        ya = _attn_call(_attn_a_kernel, "attn_gqa", qa, ka, va, (), 2, RC_GQA)
        lambda_init = 0.8 - 0.6 * math.exp(-0.3 * l)
        yb = _attn_call(_attn_b_kernel, "attn_diff", qb, kb, vb,
                        (b_lambda[l], jnp.full((1, LANES), lambda_init, F32),
                         jnp.tile(b_subln[l], 2).reshape(1, LANES)), 2, RC_DIFF)
        yc = _attn_call(_attn_c_kernel, "attn_mla", qc, kc, vc, (), 2, RC_MLA)

        x2 = _merge_call(x2, g1, ya, yb, yc, w_gate, wb, w_out[l])
        x2 = _mlp_call(x2, ln2_g[l].reshape(1, d), w_ff1[l], w_ff2[l], gf, l == DEPTH - 1)
    return x2.reshape(b, s, d)
```

```python
import functools
import math

import jax
import jax.numpy as jnp
from jax import lax
from jax.experimental import pallas as pl
from jax.experimental.pallas import tpu as pltpu

F32 = jnp.float32
BF16 = jnp.bfloat16

D_MODEL = 1024
DEPTH = 4
GRID_W = 64
EPS = 1e-6

A_HEADS = 8
A_KV_HEADS = 2
A_HEAD_DIM = 64
AXIAL_THETA = 10000.0

B_HEADS = 8
B_QK_DIM = 32
B_V_DIM = 64
B_ROT_DIM = 8
ROPE_THETA = 500000.0

C_HEADS = 8
C_Q_RANK = 384
C_KV_RANK = 256
C_NOPE_DIM = 64
C_ROPE_DIM = 32
C_V_DIM = 64
MLA_THETA = 10000.0

N_BRANCH = 3
BRANCH_W = 512
D_FF = 4 * D_MODEL

LANES = 128
LOG2E = math.log2(math.e)
VMEM_LIMIT = 52 * 1024 * 1024
OUT_GROUPS = BRANCH_W // LANES

OFF_CQ, OFF_CKV, OFF_CKR = 0, 384, 640
OFF_AQ, OFF_AK = 768, 1280
OFF_BQ, OFF_BK = 1408, 1920
OFF_AV, OFF_BV = 2432, 2560
PROJ_COLS = 3072

TM_PROJ = 512
SUB_PROJ = 256
TM_MERGE = 512
TM_MLP = 512
RC_GQA = 1024
RC_DIFF = 512
RC_MLA = 512

A_HEAD_ORDER = (0, 4, 1, 5, 2, 6, 3, 7)


def _cparams(n_axes):
    return pltpu.CompilerParams(
        dimension_semantics=("parallel",) * n_axes,
        vmem_limit_bytes=VMEM_LIMIT,
    )


def _const_spec(shape):
    nd = len(shape)
    return pl.BlockSpec(shape, lambda *_: (0,) * nd, pipeline_mode=pl.Buffered(1))


def _rms(x, g):
    return x * lax.rsqrt(jnp.mean(x * x, axis=-1, keepdims=True) + EPS) * g


def _lane_ids(shape):
    return lax.broadcasted_iota(jnp.int32, shape, len(shape) - 1)


def _rope_block(x, cos, sin_signed, first, shift):
    up = pltpu.roll(x, LANES - shift, 1)
    dn = pltpu.roll(x, shift, 1)
    return x * cos + jnp.where(first, up, dn) * sin_signed


def _rope(x, cos, sin_signed, first, shift):
    blocks = [
        _rope_block(x[:, i * LANES:(i + 1) * LANES], cos, sin_signed, first, shift)
        for i in range(x.shape[1] // LANES)
    ]
    return blocks[0] if len(blocks) == 1 else jnp.concatenate(blocks, axis=1)


def _group_sum(x, gmat):
    hi = x.astype(BF16)
    lo = (x - hi.astype(F32)).astype(BF16)
    return (jnp.dot(hi, gmat, preferred_element_type=F32)
            + jnp.dot(lo, gmat, preferred_element_type=F32))


def _store_groups(ref, rows, x):
    for g in range(x.shape[1] // LANES):
        ref[0, g, rows, :] = x[:, g * LANES:(g + 1) * LANES].astype(ref.dtype)


def _store_groups_t(ref, rows, x):
    for g in range(x.shape[1] // LANES):
        ref[0, g, :, rows] = x[:, g * LANES:(g + 1) * LANES].T.astype(ref.dtype)


def _store_values(ref, rows, v):
    ones = jnp.ones((v.shape[0], LANES), ref.dtype)
    for g in range(v.shape[1] // LANES):
        ref[0, g, rows, :] = jnp.concatenate(
            [v[:, g * LANES:(g + 1) * LANES].astype(ref.dtype), ones], axis=1)


def _proj_kernel(x_ref, g1_ref, w_ref, aqg_ref, akg_ref, cqg_ref, ckvg_ref,
                 wuq_ref, wkn_ref, wv_ref, gmat_ref,
                 cosa_ref, sina_ref, cosb_ref, sinb_ref, cosc_ref, sinc_ref,
                 qa_ref, ka_ref, va_ref, qb_ref, kb_ref, vb_ref,
                 qc_ref, kc_ref, vc_ref):
    lane = _lane_ids((1, LANES))
    first_a = (lane % 32) < 16
    first_b = (lane % 32) < 4
    first_c = (lane >= 64) & (lane < 80)
    gmat = gmat_ref[...]

    def head_norm(t, g, gm):
        ms = _group_sum(t * t, gm) * (1.0 / A_HEAD_DIM)
        return t * lax.rsqrt(ms + EPS) * g

    for sub in range(x_ref.shape[0] // SUB_PROJ):
        rows = pl.ds(sub * SUB_PROJ, SUB_PROJ)
        h = _rms(x_ref[rows, :], g1_ref[...]).astype(BF16)
        proj = jnp.dot(h, w_ref[...], preferred_element_type=F32)

        def p(off, width):
            return proj[:, off:off + width]

        cosa, sina = cosa_ref[rows, :], sina_ref[rows, :]
        cosb, sinb = cosb_ref[rows, :], sinb_ref[rows, :]
        cosc, sinc = cosc_ref[rows, :], sinc_ref[rows, :]

        cq = _rms(p(OFF_CQ, C_Q_RANK), cqg_ref[...]).astype(BF16)
        q_up = jnp.dot(cq, wuq_ref[...], preferred_element_type=F32)
        _store_groups(qc_ref, rows, _rope(q_up, cosc, sinc, first_c, 16)
                      * ((C_NOPE_DIM + C_ROPE_DIM) ** -0.5 * LOG2E))
        ckv = _rms(p(OFF_CKV, C_KV_RANK), ckvg_ref[...]).astype(BF16)
        k_nope = jnp.dot(ckv, wkn_ref[...], preferred_element_type=F32)
        k_rope = _rope(p(OFF_CKR, LANES), cosc, sinc, first_c, 16)
        _store_groups_t(kc_ref, rows,
                        k_nope + jnp.concatenate([k_rope] * C_HEADS, axis=1))
        _store_values(vc_ref, rows, jnp.dot(ckv, wv_ref[...], preferred_element_type=F32))

        aq = head_norm(p(OFF_AQ, 512), aqg_ref[...], gmat)
        _store_groups(qa_ref, rows, _rope(aq, cosa, sina, first_a, 16)
                      * (A_HEAD_DIM ** -0.5 * LOG2E))
        ak = head_norm(p(OFF_AK, 128), akg_ref[...], gmat[:LANES, :LANES])
        _store_groups_t(ka_ref, rows, _rope(ak, cosa, sina, first_a, 16))

        _store_groups(qb_ref, rows, _rope(p(OFF_BQ, 512), cosb, sinb, first_b, 4)
                      * (B_QK_DIM ** -0.5 * LOG2E))
        _store_groups_t(kb_ref, rows,
                        _rope(p(OFF_BK, 512), cosb, sinb, first_b, 4))

        _store_values(va_ref, rows, p(OFF_AV, 128))
        _store_values(vb_ref, rows, p(OFF_BV, 512))


def _proj_call(x2, g1, w_main, aqg, akg, cqg, ckvg, wuq, wkn, wv, gmat, tables, batch, seq):
    n = x2.shape[0]
    tm = TM_PROJ
    nt = seq // tm
    tab_spec = pl.BlockSpec((tm, LANES), lambda i: (i % nt, 0))

    def rows(g, w):
        return (jax.ShapeDtypeStruct((batch, g, seq, w), BF16),
                pl.BlockSpec((1, g, tm, w), lambda i: (i // nt, 0, i % nt, 0)))

    def cols(g):
        return (jax.ShapeDtypeStruct((batch, g, LANES, seq), BF16),
                pl.BlockSpec((1, g, LANES, tm), lambda i: (i // nt, 0, 0, i % nt)))

    outs = [rows(4, LANES), cols(1), rows(1, 2 * LANES),
            rows(4, LANES), cols(4), rows(4, 2 * LANES),
            rows(8, LANES), cols(8), rows(4, 2 * LANES)]
    return pl.pallas_call(
        _proj_kernel,
        grid=(n // tm,),
        in_specs=[pl.BlockSpec((tm, D_MODEL), lambda i: (i, 0)),
                  _const_spec(g1.shape), _const_spec(w_main.shape),
                  _const_spec(aqg.shape), _const_spec(akg.shape),
                  _const_spec(cqg.shape), _const_spec(ckvg.shape),
                  _const_spec(wuq.shape), _const_spec(wkn.shape), _const_spec(wv.shape),
                  _const_spec(gmat.shape)] + [tab_spec] * 6,
        out_specs=[o[1] for o in outs],
        out_shape=[o[0] for o in outs],
        compiler_params=_cparams(1),
        name="proj_prep",
    )(x2, g1, w_main, aqg, akg, cqg, ckvg, wuq, wkn, wv, gmat, *tables)


def _run_pipeline(n_items, stage1, stage2, stage3):
    assert n_items % 2 == 0
    i32 = jnp.int32
    stage1(i32(0), 0)
    stage1(i32(1), 1)
    stage2(0)

    def body(u, carry):
        t = 2 * u + 1
        stage1(t + 1, 0)
        stage2(1)
        stage1(t + 2, 1)
        stage3(t - 1, 0)
        stage2(0)
        stage3(t, 1)
        return carry

    lax.fori_loop(0, (n_items - 2) // 2, body, 0)
    stage2(1)
    stage3(i32(n_items - 2), 0)
    stage3(i32(n_items - 1), 1)


def _item_locator(n_rows, rows_per_item, halves):
    n_chunks = n_rows // rows_per_item
    assert n_chunks & (n_chunks - 1) == 0 and halves in (1, 2)
    shift = n_chunks.bit_length() - 1

    def locate(t):
        u = t >> (halves - 1)
        c = u & (n_chunks - 1)
        return u >> shift, pl.ds(pl.multiple_of(c * rows_per_item, rows_per_item),
                                 rows_per_item)

    return locate, halves * OUT_GROUPS * n_chunks


def _attn_pipeline(rows_per_item, n_sub, halves, get_qk, get_v, finalize, o_ref,
                   s_buf, m_buf, e_buf):
    locate, n_items = _item_locator(o_ref.shape[2], rows_per_item, halves)

    def stage1(t, slot):
        g, rows = locate(t)
        for j in range(n_sub):
            q, kt = get_qk(g, rows, j, slot)
            s = jnp.dot(q, kt, preferred_element_type=F32)
            s_buf[slot, j] = s
            m_buf[slot, j] = jnp.max(s, axis=-1, keepdims=True)

    def stage2(slot):
        for j in range(n_sub):
            e_buf[slot, j] = jnp.exp2(s_buf[slot, j] - m_buf[slot, j]).astype(BF16)

    def stage3(t, slot):
        g, rows = locate(t)
        outs = []
        for j in range(n_sub):
            r = jnp.dot(e_buf[slot, j], get_v(g, j), preferred_element_type=F32)
            outs.append(r[:, :LANES] / r[:, LANES:])
        o = finalize(outs, slot).astype(o_ref.dtype)
        if halves == 1:
            o_ref[0, g, rows, :] = o
        else:
            lo = slot * (LANES // 2)
            o_ref[0, g, rows, lo:lo + LANES // 2] = o[:, lo:lo + LANES // 2]

    _run_pipeline(n_items, stage1, stage2, stage3)


def _attn_a_kernel(q_ref, k_ref, v_ref, o_ref, s_buf, m_buf, e_buf, *, rc):
    lane = _lane_ids((1, LANES))
    low = lane < 64

    def get_qk(g, rows, j, half):
        q = q_ref[0, g, rows, :]
        zero = jnp.zeros_like(q)
        return (jnp.where(low, q, zero) if half == 0 else jnp.where(low, zero, q)), k_ref[0, 0]

    def get_v(g, j):
        return v_ref[0, 0]

    def finalize(outs, half):
        return outs[0]

    _attn_pipeline(rc, 1, 2, get_qk, get_v, finalize, o_ref, s_buf, m_buf, e_buf)


def _attn_b_kernel(q_ref, k_ref, v_ref, lam_ref, linit_ref, subg_ref, o_ref,
                   s_buf, m_buf, e_buf, *, rc):
    lane = _lane_ids((1, LANES))
    lf = lam_ref[...]
    lam = (jnp.exp(jnp.sum(lf[0:1] * lf[1:2], axis=-1, keepdims=True))
           - jnp.exp(jnp.sum(lf[2:3] * lf[3:4], axis=-1, keepdims=True))
           + linit_ref[:, 0:1])
    post = subg_ref[...] * (1.0 - linit_ref[...])

    def get_qk(g, rows, j, half):
        q = q_ref[0, g, rows, :]
        lo = half * B_V_DIM + j * B_QK_DIM
        sel = (lane >= lo) & (lane < lo + B_QK_DIM)
        return jnp.where(sel, q, jnp.zeros_like(q)), k_ref[0, g]

    def get_v(g, j):
        return v_ref[0, g]

    def finalize(outs, half):
        o = outs[0] - lam * outs[1]
        in_head = (lane >= half * B_V_DIM) & (lane < (half + 1) * B_V_DIM)
        ms = jnp.sum(jnp.where(in_head, o * o, 0.0), axis=-1,
                     keepdims=True) * (1.0 / B_V_DIM)
        return o * lax.rsqrt(ms + EPS) * post

    _attn_pipeline(rc, 2, 2, get_qk, get_v, finalize, o_ref, s_buf, m_buf, e_buf)


def _attn_c_kernel(q_ref, k_ref, v_ref, o_ref, s_buf, m_buf, e_buf, *, rc):
    def get_qk(g, rows, j, half):
        return q_ref[0, 2 * g + half, rows, :], k_ref[0, 2 * g + half]

    def get_v(g, j):
        return v_ref[0, g]

    def finalize(outs, half):
        return outs[0]

    _attn_pipeline(rc, 1, 2, get_qk, get_v, finalize, o_ref, s_buf, m_buf, e_buf)


def _softmax_scratch(n_sub, rc, s):
    return [pltpu.VMEM((2, n_sub, rc, s), F32),
            pltpu.VMEM((2, n_sub, rc, 1), F32),
            pltpu.VMEM((2, n_sub, rc, s), BF16)]


def _attn_call(kernel, name, q, kt, v, extra, n_sub, rc):
    b, _, s, _ = q.shape
    rc = min(rc, s)

    def batch_spec(a):
        return pl.BlockSpec((1,) + a.shape[1:], lambda bi: (bi, 0, 0, 0))

    return pl.pallas_call(
        functools.partial(kernel, rc=rc),
        grid=(b,),
        in_specs=[batch_spec(q), batch_spec(kt), batch_spec(v)]
        + [_const_spec(e.shape) for e in extra],
        out_specs=pl.BlockSpec((1, OUT_GROUPS, s, LANES), lambda bi: (bi, 0, 0, 0)),
        out_shape=jax.ShapeDtypeStruct((b, OUT_GROUPS, s, LANES), BF16),
        scratch_shapes=_softmax_scratch(n_sub, rc, s),
        compiler_params=_cparams(1),
        name=name,
    )(q, kt, v, *extra)


def _merge_kernel(x_ref, g1_ref, ya_ref, yb_ref, yc_ref, wg_ref, wb_ref, wo_ref, o_ref):
    x = x_ref[...]
    h = _rms(x, g1_ref[...]).astype(BF16)
    merged = None
    for n, y_ref in enumerate((ya_ref, yb_ref, yc_ref)):
        logits = jnp.dot(h, wg_ref[:, n * D_MODEL:(n + 1) * D_MODEL],
                         preferred_element_type=F32)
        y = jnp.concatenate([y_ref[0, g] for g in range(OUT_GROUPS)], axis=1)
        z = jnp.dot(y, wb_ref[n], preferred_element_type=F32)
        term = jax.nn.sigmoid(logits) * z
        merged = term if merged is None else merged + term
    o_ref[...] = x + jnp.dot(merged.astype(BF16), wo_ref[...],
                             preferred_element_type=F32)


def _merge_call(x2, g1, ya, yb, yc, wg, wb, wo):
    n = x2.shape[0]
    tm = TM_MERGE
    nt = ya.shape[2] // tm
    tok = pl.BlockSpec((tm, D_MODEL), lambda i: (i, 0))
    y_spec = pl.BlockSpec((1, OUT_GROUPS, tm, LANES), lambda i: (i // nt, 0, i % nt, 0))
    return pl.pallas_call(
        _merge_kernel,
        grid=(n // tm,),
        in_specs=[tok, _const_spec(g1.shape), y_spec, y_spec, y_spec,
                  _const_spec(wg.shape), _const_spec(wb.shape), _const_spec(wo.shape)],
        out_specs=tok,
        out_shape=jax.ShapeDtypeStruct((n, D_MODEL), F32),
        compiler_params=_cparams(1),
        name="gated_merge",
    )(x2, g1, ya, yb, yc, wg, wb, wo)


FF_CHUNK = 1024


def _mlp_kernel(x_ref, g2_ref, w1_ref, w2_ref, gf_ref, o_ref, *, final):
    x = x_ref[...]
    h = _rms(x, g2_ref[...]).astype(BF16)
    acc = x
    for c in range(D_FF // FF_CHUNK):
        cols = pl.ds(c * FF_CHUNK, FF_CHUNK)
        hid = jnp.dot(h, w1_ref[:, cols], preferred_element_type=F32)
        hid = jnp.square(jnp.maximum(hid, 0.0)).astype(BF16)
        acc = acc + jnp.dot(hid, w2_ref[cols, :], preferred_element_type=F32)
    if final:
        acc = _rms(acc, gf_ref[...])
    o_ref[...] = acc


def _mlp_call(x2, g2, w1, w2, gf, final):
    n = x2.shape[0]
    tm = TM_MLP
    tok = pl.BlockSpec((tm, D_MODEL), lambda i: (i, 0))
    return pl.pallas_call(
        functools.partial(_mlp_kernel, final=final),
        grid=(n // tm,),
        in_specs=[tok, _const_spec(g2.shape), _const_spec(w1.shape),
                  _const_spec(w2.shape), _const_spec(gf.shape)],
        out_specs=tok,
        out_shape=jax.ShapeDtypeStruct((n, D_MODEL), F32),
        compiler_params=_cparams(1),
        name="mlp_final" if final else "mlp",
    )(x2, g2, w1, w2, gf)


def _angles(pos, dim, theta):
    inv_freq = theta ** (-jnp.arange(0, dim, 2, dtype=F32) / dim)
    return pos.astype(F32)[:, None] * inv_freq[None, :]


def _rope_tables(seq):
    t = jnp.arange(seq)
    half = A_HEAD_DIM // 2
    row_ang = _angles(t // GRID_W, half, AXIAL_THETA)
    col_ang = _angles(t % GRID_W, half, AXIAL_THETA)
    b_ang = _angles(t, B_ROT_DIM, ROPE_THETA)
    c_ang = _angles(t, C_ROPE_DIM, MLA_THETA)

    def pair(ang):
        c, s = jnp.cos(ang), jnp.sin(ang)
        return jnp.concatenate([c, c], 1), jnp.concatenate([-s, s], 1)

    one = lambda w: jnp.ones((seq, w), F32)
    zero = lambda w: jnp.zeros((seq, w), F32)

    rc, rs = pair(row_ang)
    cc, cs = pair(col_ang)
    cos_a = jnp.concatenate([rc, cc] * 2, 1)
    sin_a = jnp.concatenate([rs, cs] * 2, 1)

    bc, bs = pair(b_ang)
    cos_b = jnp.concatenate([bc, one(B_QK_DIM - B_ROT_DIM)] * 4, 1)
    sin_b = jnp.concatenate([bs, zero(B_QK_DIM - B_ROT_DIM)] * 4, 1)

    mc, ms = pair(c_ang)
    cos_c = jnp.concatenate([one(64), mc, one(32)], 1)
    sin_c = jnp.concatenate([zero(64), ms, zero(32)], 1)
    return cos_a, sin_a, cos_b, sin_b, cos_c, sin_c


def _layer_weights(l, w_in, c_w_uq, c_w_ukv, w_branch):
    w = w_in[l]
    a_q = w[:, 0:512].reshape(D_MODEL, A_HEADS, A_HEAD_DIM)
    a_q = a_q[:, jnp.array(A_HEAD_ORDER), :].reshape(D_MODEL, 512)
    c_kr = jnp.zeros((D_MODEL, LANES), w.dtype).at[:, 64:96].set(w[:, 2944:2976])
    w_main = jnp.concatenate(
        [w[:, 2304:2944], c_kr, a_q, w[:, 512:640], w[:, 768:1792],
         w[:, 640:768], w[:, 1792:2304]], axis=1)
    assert w_main.shape[1] == PROJ_COLS
    w_gate = w[:, 2976:]

    uq = c_w_uq[l].reshape(C_Q_RANK, C_HEADS, C_NOPE_DIM + C_ROPE_DIM)
    uq = jnp.pad(uq, ((0, 0), (0, 0), (0, LANES - C_NOPE_DIM - C_ROPE_DIM)))
    wuq = uq.reshape(C_Q_RANK, C_HEADS * LANES)
    ukv = c_w_ukv[l].reshape(C_KV_RANK, C_HEADS, C_NOPE_DIM + C_V_DIM)
    kn = jnp.pad(ukv[:, :, :C_NOPE_DIM], ((0, 0), (0, 0), (0, LANES - C_NOPE_DIM)))
    wkn = kn.reshape(C_KV_RANK, C_HEADS * LANES)
    wv = ukv[:, :, C_NOPE_DIM:].reshape(C_KV_RANK, C_HEADS * C_V_DIM)

    wb = w_branch[l]
    wb_a = wb[0].reshape(A_HEADS, A_HEAD_DIM, D_MODEL)[jnp.array(A_HEAD_ORDER)]
    wb = jnp.stack([wb_a.reshape(BRANCH_W, D_MODEL), wb[1], wb[2]])
    return w_main, w_gate, wuq, wkn, wv, wb


def kernel(x, ln1_g, w_in, a_q_norm, a_k_norm, b_lambda, b_subln, c_q_norm, c_kv_norm,
           c_w_uq, c_w_ukv, w_branch, w_out, ln2_g, w_ff1, w_ff2, final_g):
    b, s, d = x.shape
    n = b * s
    tables = _rope_tables(s)
    gidx = jnp.arange(BRANCH_W) // A_HEAD_DIM
    gmat = (gidx[:, None] == gidx[None, :]).astype(BF16)
    gf = final_g.reshape(1, d)
    w_in, c_w_uq, c_w_ukv, w_branch, w_out, w_ff1, w_ff2 = (
        t.astype(BF16) for t in (w_in, c_w_uq, c_w_ukv, w_branch, w_out, w_ff1, w_ff2))

    x2 = x.reshape(n, d)
    for l in range(DEPTH):
        w_main, w_gate, wuq, wkn, wv, wb = _layer_weights(l, w_in, c_w_uq, c_w_ukv, w_branch)
        g1 = ln1_g[l].reshape(1, d)
        qa, ka, va, qb, kb, vb, qc, kc, vc = _proj_call(
            x2, g1, w_main,
            jnp.tile(a_q_norm[l], A_HEADS).reshape(1, 512),
            jnp.tile(a_k_norm[l], A_KV_HEADS).reshape(1, 128),
            c_q_norm[l].reshape(1, C_Q_RANK), c_kv_norm[l].reshape(1, C_KV_RANK),
            wuq, wkn, wv, gmat, tables, b, s)

        ya = _attn_call(_attn_a_kernel, "attn_gqa", qa, ka, va, (), 1, RC_GQA)
        lambda_init = 0.8 - 0.6 * math.exp(-0.3 * l)
        yb = _attn_call(_attn_b_kernel, "attn_diff", qb, kb, vb,
                        (b_lambda[l], jnp.full((1, LANES), lambda_init, F32),
                         jnp.tile(b_subln[l], 2).reshape(1, LANES)), 2, RC_DIFF)
        yc = _attn_call(_attn_c_kernel, "attn_mla", qc, kc, vc, (), 1, RC_MLA)

        x2 = _merge_call(x2, g1, ya, yb, yc, w_gate, wb, w_out[l])
        x2 = _mlp_call(x2, ln2_g[l].reshape(1, d), w_ff1[l], w_ff2[l], gf, l == DEPTH - 1)
    return x2.reshape(b, s, d)
```

```python
import functools
import math

import jax
import jax.numpy as jnp
from jax import lax
from jax.experimental import pallas as pl
from jax.experimental.pallas import tpu as pltpu

F32 = jnp.float32
BF16 = jnp.bfloat16

D_MODEL = 1024
DEPTH = 4
GRID_W = 64
EPS = 1e-6

A_HEADS = 8
A_KV_HEADS = 2
A_HEAD_DIM = 64
AXIAL_THETA = 10000.0

B_HEADS = 8
B_QK_DIM = 32
B_V_DIM = 64
B_ROT_DIM = 8
ROPE_THETA = 500000.0

C_HEADS = 8
C_Q_RANK = 384
C_KV_RANK = 256
C_NOPE_DIM = 64
C_ROPE_DIM = 32
C_V_DIM = 64
MLA_THETA = 10000.0

N_BRANCH = 3
BRANCH_W = 512
D_FF = 4 * D_MODEL

LANES = 128
LOG2E = math.log2(math.e)
VMEM_LIMIT = 52 * 1024 * 1024
OUT_GROUPS = BRANCH_W // LANES

OFF_CQ, OFF_CKV, OFF_CKR = 0, 384, 640
OFF_AQ, OFF_AK = 768, 1280
OFF_BQ, OFF_BK = 1408, 1920
OFF_AV, OFF_BV = 2432, 2560
PROJ_COLS = 3072

TM_PROJ = 512
SUB_PROJ = 256
TM_MERGE = 1024
TM_MLP = 1024
RC_GQA = 512
RC_DIFF = 512
RC_MLA = 512

A_HEAD_ORDER = (0, 4, 1, 5, 2, 6, 3, 7)


def _cparams(n_axes):
    return pltpu.CompilerParams(
        dimension_semantics=("parallel",) * n_axes,
        vmem_limit_bytes=VMEM_LIMIT,
    )


def _const_spec(shape):
    nd = len(shape)
    return pl.BlockSpec(shape, lambda *_: (0,) * nd, pipeline_mode=pl.Buffered(1))


def _rms(x, g):
    return x * lax.rsqrt(jnp.mean(x * x, axis=-1, keepdims=True) + EPS) * g


def _lane_ids(shape):
    return lax.broadcasted_iota(jnp.int32, shape, len(shape) - 1)


def _rope_block(x, cos, sin_signed, first, shift):
    up = pltpu.roll(x, LANES - shift, 1)
    dn = pltpu.roll(x, shift, 1)
    return x * cos + jnp.where(first, up, dn) * sin_signed


def _rope(x, cos, sin_signed, first, shift):
    blocks = [
        _rope_block(x[:, i * LANES:(i + 1) * LANES], cos, sin_signed, first, shift)
        for i in range(x.shape[1] // LANES)
    ]
    return blocks[0] if len(blocks) == 1 else jnp.concatenate(blocks, axis=1)


def _group_sum(x, gmat):
    hi = x.astype(BF16)
    lo = (x - hi.astype(F32)).astype(BF16)
    return (jnp.dot(hi, gmat, preferred_element_type=F32)
            + jnp.dot(lo, gmat, preferred_element_type=F32))


def _store_groups(ref, rows, x):
    for g in range(x.shape[1] // LANES):
        ref[0, g, rows, :] = x[:, g * LANES:(g + 1) * LANES].astype(ref.dtype)


def _store_groups_t(ref, rows, x):
    for g in range(x.shape[1] // LANES):
        ref[0, g, :, rows] = x[:, g * LANES:(g + 1) * LANES].T.astype(ref.dtype)


def _store_values(ref, rows, v):
    ones = jnp.ones((v.shape[0], LANES), ref.dtype)
    for g in range(v.shape[1] // LANES):
        ref[0, g, rows, :] = jnp.concatenate(
            [v[:, g * LANES:(g + 1) * LANES].astype(ref.dtype), ones], axis=1)


def _proj_kernel(x_ref, g1_ref, w_ref, aqg_ref, akg_ref, cqg_ref, ckvg_ref,
                 wuq_ref, wkn_ref, wv_ref, gmat_ref,
                 cosa_ref, sina_ref, cosb_ref, sinb_ref, cosc_ref, sinc_ref,
                 qa_ref, ka_ref, va_ref, qb_ref, kb_ref, vb_ref,
                 qc_ref, kc_ref, vc_ref):
    lane = _lane_ids((1, LANES))
    first_a = (lane % 32) < 16
    first_b = (lane % 32) < 4
    first_c = (lane >= 64) & (lane < 80)
    gmat = gmat_ref[...]

    def head_norm(t, g, gm):
        ms = _group_sum(t * t, gm) * (1.0 / A_HEAD_DIM)
        return t * lax.rsqrt(ms + EPS) * g

    for sub in range(x_ref.shape[0] // SUB_PROJ):
        rows = pl.ds(sub * SUB_PROJ, SUB_PROJ)
        h = _rms(x_ref[rows, :], g1_ref[...]).astype(BF16)
        proj = jnp.dot(h, w_ref[...], preferred_element_type=F32)

        def p(off, width):
            return proj[:, off:off + width]

        cosa, sina = cosa_ref[rows, :], sina_ref[rows, :]
        cosb, sinb = cosb_ref[rows, :], sinb_ref[rows, :]
        cosc, sinc = cosc_ref[rows, :], sinc_ref[rows, :]

        cq = _rms(p(OFF_CQ, C_Q_RANK), cqg_ref[...]).astype(BF16)
        q_up = jnp.dot(cq, wuq_ref[...], preferred_element_type=F32)
        _store_groups(qc_ref, rows, _rope(q_up, cosc, sinc, first_c, 16)
                      * ((C_NOPE_DIM + C_ROPE_DIM) ** -0.5 * LOG2E))
        ckv = _rms(p(OFF_CKV, C_KV_RANK), ckvg_ref[...]).astype(BF16)
        k_nope = jnp.dot(ckv, wkn_ref[...], preferred_element_type=F32)
        k_rope = _rope(p(OFF_CKR, LANES), cosc, sinc, first_c, 16)
        _store_groups_t(kc_ref, rows,
                        k_nope + jnp.concatenate([k_rope] * C_HEADS, axis=1))
        _store_values(vc_ref, rows, jnp.dot(ckv, wv_ref[...], preferred_element_type=F32))

        aq = head_norm(p(OFF_AQ, 512), aqg_ref[...], gmat)
        _store_groups(qa_ref, rows, _rope(aq, cosa, sina, first_a, 16)
                      * (A_HEAD_DIM ** -0.5 * LOG2E))
        ak = head_norm(p(OFF_AK, 128), akg_ref[...], gmat[:LANES, :LANES])
        _store_groups_t(ka_ref, rows, _rope(ak, cosa, sina, first_a, 16))

        _store_groups(qb_ref, rows, _rope(p(OFF_BQ, 512), cosb, sinb, first_b, 4)
                      * (B_QK_DIM ** -0.5 * LOG2E))
        _store_groups_t(kb_ref, rows,
                        _rope(p(OFF_BK, 512), cosb, sinb, first_b, 4))

        _store_values(va_ref, rows, p(OFF_AV, 128))
        _store_values(vb_ref, rows, p(OFF_BV, 512))


def _proj_call(x2, g1, w_main, aqg, akg, cqg, ckvg, wuq, wkn, wv, gmat, tables, batch, seq):
    n = x2.shape[0]
    tm = TM_PROJ
    nt = seq // tm
    tab_spec = pl.BlockSpec((tm, LANES), lambda i: (i % nt, 0))

    def rows(g, w):
        return (jax.ShapeDtypeStruct((batch, g, seq, w), BF16),
                pl.BlockSpec((1, g, tm, w), lambda i: (i // nt, 0, i % nt, 0)))

    def cols(g):
        return (jax.ShapeDtypeStruct((batch, g, LANES, seq), BF16),
                pl.BlockSpec((1, g, LANES, tm), lambda i: (i // nt, 0, 0, i % nt)))

    outs = [rows(4, LANES), cols(1), rows(1, 2 * LANES),
            rows(4, LANES), cols(4), rows(4, 2 * LANES),
            rows(8, LANES), cols(8), rows(4, 2 * LANES)]
    return pl.pallas_call(
        _proj_kernel,
        grid=(n // tm,),
        in_specs=[pl.BlockSpec((tm, D_MODEL), lambda i: (i, 0)),
                  _const_spec(g1.shape), _const_spec(w_main.shape),
                  _const_spec(aqg.shape), _const_spec(akg.shape),
                  _const_spec(cqg.shape), _const_spec(ckvg.shape),
                  _const_spec(wuq.shape), _const_spec(wkn.shape), _const_spec(wv.shape),
                  _const_spec(gmat.shape)] + [tab_spec] * 6,
        out_specs=[o[1] for o in outs],
        out_shape=[o[0] for o in outs],
        compiler_params=_cparams(1),
        name="proj_prep",
    )(x2, g1, w_main, aqg, akg, cqg, ckvg, wuq, wkn, wv, gmat, *tables)


def _run_pipeline(n_items, stage1, stage2, stage3):
    assert n_items % 2 == 0
    i32 = jnp.int32
    stage1(i32(0), 0)
    stage1(i32(1), 1)
    stage2(0)

    def body(u, carry):
        t = 2 * u + 1
        stage1(t + 1, 0)
        stage2(1)
        stage1(t + 2, 1)
        stage3(t - 1, 0)
        stage2(0)
        stage3(t, 1)
        return carry

    lax.fori_loop(0, (n_items - 2) // 2, body, 0)
    stage2(1)
    stage3(i32(n_items - 2), 0)
    stage3(i32(n_items - 1), 1)


def _item_locator(n_rows, rows_per_item, halves):
    n_chunks = n_rows // rows_per_item
    assert n_chunks & (n_chunks - 1) == 0 and halves in (1, 2)
    shift = n_chunks.bit_length() - 1

    def locate(t):
        u = t >> (halves - 1)
        c = u & (n_chunks - 1)
        return u >> shift, pl.ds(pl.multiple_of(c * rows_per_item, rows_per_item),
                                 rows_per_item)

    return locate, halves * OUT_GROUPS * n_chunks


def _attn_pipeline(rows_per_item, n_sub, halves, get_qk, get_v, finalize, o_ref,
                   s_buf, m_buf, e_buf):
    locate, n_items = _item_locator(o_ref.shape[2], rows_per_item, halves)

    def stage1(t, slot):
        g, rows = locate(t)
        for j in range(n_sub):
            q, kt = get_qk(g, rows, j, slot)
            s = jnp.dot(q, kt, preferred_element_type=F32)
            s_buf[slot, j] = s
            m_buf[slot, j] = jnp.max(s, axis=-1, keepdims=True)

    def stage2(slot):
        for j in range(n_sub):
            e_buf[slot, j] = jnp.exp2(s_buf[slot, j] - m_buf[slot, j]).astype(BF16)

    def stage3(t, slot):
        g, rows = locate(t)
        outs = []
        for j in range(n_sub):
            r = jnp.dot(e_buf[slot, j], get_v(g, j), preferred_element_type=F32)
            outs.append(r[:, :LANES] / r[:, LANES:])
        o = finalize(outs, slot).astype(o_ref.dtype)
        if halves == 1:
            o_ref[0, g, rows, :] = o
        else:
            lo = slot * (LANES // 2)
            o_ref[0, g, rows, lo:lo + LANES // 2] = o[:, lo:lo + LANES // 2]

    _run_pipeline(n_items, stage1, stage2, stage3)


def _attn_a_kernel(q_ref, k_ref, v_ref, o_ref, s_buf, m_buf, e_buf, *, rc):
    lane = _lane_ids((1, LANES))
    low = lane < 64

    def get_qk(g, rows, j, half):
        q = q_ref[0, g, rows, :]
        zero = jnp.zeros_like(q)
        return (jnp.where(low, q, zero) if j == 0 else jnp.where(low, zero, q)), k_ref[0, 0]

    def get_v(g, j):
        return v_ref[0, 0]

    def finalize(outs, half):
        return jnp.where(low, outs[0], outs[1])

    _attn_pipeline(rc, 2, 1, get_qk, get_v, finalize, o_ref, s_buf, m_buf, e_buf)


def _attn_b_kernel(q_ref, k_ref, v_ref, lam_ref, linit_ref, subg_ref, o_ref,
                   s_buf, m_buf, e_buf, *, rc):
    lane = _lane_ids((1, LANES))
    lf = lam_ref[...]
    lam = (jnp.exp(jnp.sum(lf[0:1] * lf[1:2], axis=-1, keepdims=True))
           - jnp.exp(jnp.sum(lf[2:3] * lf[3:4], axis=-1, keepdims=True))
           + linit_ref[:, 0:1])
    post = subg_ref[...] * (1.0 - linit_ref[...])

    def get_qk(g, rows, j, half):
        q = q_ref[0, g, rows, :]
        lo = half * B_V_DIM + j * B_QK_DIM
        sel = (lane >= lo) & (lane < lo + B_QK_DIM)
        return jnp.where(sel, q, jnp.zeros_like(q)), k_ref[0, g]

    def get_v(g, j):
        return v_ref[0, g]

    def finalize(outs, half):
        o = outs[0] - lam * outs[1]
        in_head = (lane >= half * B_V_DIM) & (lane < (half + 1) * B_V_DIM)
        ms = jnp.sum(jnp.where(in_head, o * o, 0.0), axis=-1,
                     keepdims=True) * (1.0 / B_V_DIM)
        return o * lax.rsqrt(ms + EPS) * post

    _attn_pipeline(rc, 2, 2, get_qk, get_v, finalize, o_ref, s_buf, m_buf, e_buf)


def _attn_c_kernel(q_ref, k_ref, v_ref, o_ref, s_buf, m_buf, e_buf, *, rc):
    def get_qk(g, rows, j, half):
        return q_ref[0, 2 * g + half, rows, :], k_ref[0, 2 * g + half]

    def get_v(g, j):
        return v_ref[0, g]

    def finalize(outs, half):
        return outs[0]

    _attn_pipeline(rc, 1, 2, get_qk, get_v, finalize, o_ref, s_buf, m_buf, e_buf)


def _softmax_scratch(n_sub, rc, s):
    return [pltpu.VMEM((2, n_sub, rc, s), F32),
            pltpu.VMEM((2, n_sub, rc, 1), F32),
            pltpu.VMEM((2, n_sub, rc, s), BF16)]


def _attn_call(kernel, name, q, kt, v, extra, n_sub, rc):
    b, _, s, _ = q.shape
    rc = min(rc, s)

    def batch_spec(a):
        return pl.BlockSpec((1,) + a.shape[1:], lambda bi: (bi, 0, 0, 0))

    return pl.pallas_call(
        functools.partial(kernel, rc=rc),
        grid=(b,),
        in_specs=[batch_spec(q), batch_spec(kt), batch_spec(v)]
        + [_const_spec(e.shape) for e in extra],
        out_specs=pl.BlockSpec((1, OUT_GROUPS, s, LANES), lambda bi: (bi, 0, 0, 0)),
        out_shape=jax.ShapeDtypeStruct((b, OUT_GROUPS, s, LANES), BF16),
        scratch_shapes=_softmax_scratch(n_sub, rc, s),
        compiler_params=_cparams(1),
        name=name,
    )(q, kt, v, *extra)


def _merge_kernel(x_ref, g1_ref, ya_ref, yb_ref, yc_ref, wg_ref, wb_ref, wo_ref, o_ref):
    x = x_ref[...]
    h = _rms(x, g1_ref[...]).astype(BF16)
    merged = None
    for n, y_ref in enumerate((ya_ref, yb_ref, yc_ref)):
        logits = jnp.dot(h, wg_ref[:, n * D_MODEL:(n + 1) * D_MODEL],
                         preferred_element_type=F32)
        y = jnp.concatenate([y_ref[0, g] for g in range(OUT_GROUPS)], axis=1)
        z = jnp.dot(y, wb_ref[n], preferred_element_type=F32)
        term = jax.nn.sigmoid(logits) * z
        merged = term if merged is None else merged + term
    o_ref[...] = x + jnp.dot(merged.astype(BF16), wo_ref[...],
                             preferred_element_type=F32)


def _merge_call(x2, g1, ya, yb, yc, wg, wb, wo):
    n = x2.shape[0]
    tm = min(TM_MERGE, ya.shape[2])
    nt = ya.shape[2] // tm
    tok = pl.BlockSpec((tm, D_MODEL), lambda i: (i, 0))
    y_spec = pl.BlockSpec((1, OUT_GROUPS, tm, LANES), lambda i: (i // nt, 0, i % nt, 0))
    return pl.pallas_call(
        _merge_kernel,
        grid=(n // tm,),
        in_specs=[tok, _const_spec(g1.shape), y_spec, y_spec, y_spec,
                  _const_spec(wg.shape), _const_spec(wb.shape), _const_spec(wo.shape)],
        out_specs=tok,
        out_shape=jax.ShapeDtypeStruct((n, D_MODEL), F32),
        compiler_params=_cparams(1),
        name="gated_merge",
    )(x2, g1, ya, yb, yc, wg, wb, wo)


FF_CHUNK = 1024


def _mlp_kernel(x_ref, g2_ref, w1_ref, w2_ref, gf_ref, o_ref, *, final):
    x = x_ref[...]
    h = _rms(x, g2_ref[...]).astype(BF16)
    acc = x
    for c in range(D_FF // FF_CHUNK):
        cols = pl.ds(c * FF_CHUNK, FF_CHUNK)
        hid = jnp.dot(h, w1_ref[:, cols], preferred_element_type=F32)
        hid = jnp.square(jnp.maximum(hid, 0.0)).astype(BF16)
        acc = acc + jnp.dot(hid, w2_ref[cols, :], preferred_element_type=F32)
    if final:
        acc = _rms(acc, gf_ref[...])
    o_ref[...] = acc


def _mlp_call(x2, g2, w1, w2, gf, final):
    n = x2.shape[0]
    tm = TM_MLP
    tok = pl.BlockSpec((tm, D_MODEL), lambda i: (i, 0))
    return pl.pallas_call(
        functools.partial(_mlp_kernel, final=final),
        grid=(n // tm,),
        in_specs=[tok, _const_spec(g2.shape), _const_spec(w1.shape),
                  _const_spec(w2.shape), _const_spec(gf.shape)],
        out_specs=tok,
        out_shape=jax.ShapeDtypeStruct((n, D_MODEL), F32),
        compiler_params=_cparams(1),
        name="mlp_final" if final else "mlp",
    )(x2, g2, w1, w2, gf)


def _angles(pos, dim, theta):
    inv_freq = theta ** (-jnp.arange(0, dim, 2, dtype=F32) / dim)
    return pos.astype(F32)[:, None] * inv_freq[None, :]


def _rope_tables(seq):
    t = jnp.arange(seq)
    half = A_HEAD_DIM // 2
    row_ang = _angles(t // GRID_W, half, AXIAL_THETA)
    col_ang = _angles(t % GRID_W, half, AXIAL_THETA)
    b_ang = _angles(t, B_ROT_DIM, ROPE_THETA)
    c_ang = _angles(t, C_ROPE_DIM, MLA_THETA)

    def pair(ang):
        c, s = jnp.cos(ang), jnp.sin(ang)
        return jnp.concatenate([c, c], 1), jnp.concatenate([-s, s], 1)

    one = lambda w: jnp.ones((seq, w), F32)
    zero = lambda w: jnp.zeros((seq, w), F32)

    rc, rs = pair(row_ang)
    cc, cs = pair(col_ang)
    cos_a = jnp.concatenate([rc, cc] * 2, 1)
    sin_a = jnp.concatenate([rs, cs] * 2, 1)

    bc, bs = pair(b_ang)
    cos_b = jnp.concatenate([bc, one(B_QK_DIM - B_ROT_DIM)] * 4, 1)
    sin_b = jnp.concatenate([bs, zero(B_QK_DIM - B_ROT_DIM)] * 4, 1)

    mc, ms = pair(c_ang)
    cos_c = jnp.concatenate([one(64), mc, one(32)], 1)
    sin_c = jnp.concatenate([zero(64), ms, zero(32)], 1)
    return cos_a, sin_a, cos_b, sin_b, cos_c, sin_c


def _layer_weights(l, w_in, c_w_uq, c_w_ukv, w_branch):
    w = w_in[l]
    a_q = w[:, 0:512].reshape(D_MODEL, A_HEADS, A_HEAD_DIM)
    a_q = a_q[:, jnp.array(A_HEAD_ORDER), :].reshape(D_MODEL, 512)
    c_kr = jnp.zeros((D_MODEL, LANES), w.dtype).at[:, 64:96].set(w[:, 2944:2976])
    w_main = jnp.concatenate(
        [w[:, 2304:2944], c_kr, a_q, w[:, 512:640], w[:, 768:1792],
         w[:, 640:768], w[:, 1792:2304]], axis=1)
    assert w_main.shape[1] == PROJ_COLS
    w_gate = w[:, 2976:]

    uq = c_w_uq[l].reshape(C_Q_RANK, C_HEADS, C_NOPE_DIM + C_ROPE_DIM)
    uq = jnp.pad(uq, ((0, 0), (0, 0), (0, LANES - C_NOPE_DIM - C_ROPE_DIM)))
    wuq = uq.reshape(C_Q_RANK, C_HEADS * LANES)
    ukv = c_w_ukv[l].reshape(C_KV_RANK, C_HEADS, C_NOPE_DIM + C_V_DIM)
    kn = jnp.pad(ukv[:, :, :C_NOPE_DIM], ((0, 0), (0, 0), (0, LANES - C_NOPE_DIM)))
    wkn = kn.reshape(C_KV_RANK, C_HEADS * LANES)
    wv = ukv[:, :, C_NOPE_DIM:].reshape(C_KV_RANK, C_HEADS * C_V_DIM)

    wb = w_branch[l]
    wb_a = wb[0].reshape(A_HEADS, A_HEAD_DIM, D_MODEL)[jnp.array(A_HEAD_ORDER)]
    wb = jnp.stack([wb_a.reshape(BRANCH_W, D_MODEL), wb[1], wb[2]])
    return w_main, w_gate, wuq, wkn, wv, wb


def kernel(x, ln1_g, w_in, a_q_norm, a_k_norm, b_lambda, b_subln, c_q_norm, c_kv_norm,
           c_w_uq, c_w_ukv, w_branch, w_out, ln2_g, w_ff1, w_ff2, final_g):
    b, s, d = x.shape
    n = b * s
    tables = _rope_tables(s)
    gidx = jnp.arange(BRANCH_W) // A_HEAD_DIM
    gmat = (gidx[:, None] == gidx[None, :]).astype(BF16)
    gf = final_g.reshape(1, d)
    w_in, c_w_uq, c_w_ukv, w_branch, w_out, w_ff1, w_ff2 = (
        t.astype(BF16) for t in (w_in, c_w_uq, c_w_ukv, w_branch, w_out, w_ff1, w_ff2))

    x2 = x.reshape(n, d)
    for l in range(DEPTH):
        w_main, w_gate, wuq, wkn, wv, wb = _layer_weights(l, w_in, c_w_uq, c_w_ukv, w_branch)
        g1 = ln1_g[l].reshape(1, d)
        qa, ka, va, qb, kb, vb, qc, kc, vc = _proj_call(
            x2, g1, w_main,
            jnp.tile(a_q_norm[l], A_HEADS).reshape(1, 512),
            jnp.tile(a_k_norm[l], A_KV_HEADS).reshape(1, 128),
            c_q_norm[l].reshape(1, C_Q_RANK), c_kv_norm[l].reshape(1, C_KV_RANK),
            wuq, wkn, wv, gmat, tables, b, s)

        ya = _attn_call(_attn_a_kernel, "attn_gqa", qa, ka, va, (), 2, RC_GQA)
        lambda_init = 0.8 - 0.6 * math.exp(-0.3 * l)
        yb = _attn_call(_attn_b_kernel, "attn_diff", qb, kb, vb,
                        (b_lambda[l], jnp.full((1, LANES), lambda_init, F32),
                         jnp.tile(b_subln[l], 2).reshape(1, LANES)), 2, RC_DIFF)
        yc = _attn_call(_attn_c_kernel, "attn_mla", qc, kc, vc, (), 1, RC_MLA)

        x2 = _merge_call(x2, g1, ya, yb, yc, w_gate, wb, w_out[l])
        x2 = _mlp_call(x2, ln2_g[l].reshape(1, d), w_ff1[l], w_ff2[l], gf, l == DEPTH - 1)
    return x2.reshape(b, s, d)
```

```python
import functools
import math

import jax
import jax.numpy as jnp
from jax import lax
from jax.experimental import pallas as pl
from jax.experimental.pallas import tpu as pltpu

F32 = jnp.float32
BF16 = jnp.bfloat16

D_MODEL = 1024
DEPTH = 4
GRID_W = 64
EPS = 1e-6

A_HEADS = 8
A_KV_HEADS = 2
A_HEAD_DIM = 64
AXIAL_THETA = 10000.0

B_HEADS = 8
B_QK_DIM = 32
B_V_DIM = 64
B_ROT_DIM = 8
ROPE_THETA = 500000.0

C_HEADS = 8
C_Q_RANK = 384
C_KV_RANK = 256
C_NOPE_DIM = 64
C_ROPE_DIM = 32
C_V_DIM = 64
MLA_THETA = 10000.0

N_BRANCH = 3
BRANCH_W = 512
D_FF = 4 * D_MODEL

LANES = 128
LOG2E = math.log2(math.e)
VMEM_LIMIT = 52 * 1024 * 1024
OUT_GROUPS = BRANCH_W // LANES

OFF_CQ, OFF_CKV, OFF_CKR = 0, 384, 640
OFF_AQ, OFF_AK = 768, 1280
OFF_BQ, OFF_BK = 1408, 1920
OFF_AV, OFF_BV = 2432, 2560
PROJ_COLS = 3072

TM_PROJ = 1024
SUB_PROJ = 512
TM_MERGE = 1024
TM_MLP = 1024
RC_GQA = 512
RC_DIFF = 512
RC_MLA = 512

A_HEAD_ORDER = (0, 4, 1, 5, 2, 6, 3, 7)


def _cparams(n_axes):
    return pltpu.CompilerParams(
        dimension_semantics=("parallel",) * n_axes,
        vmem_limit_bytes=VMEM_LIMIT,
    )


def _const_spec(shape):
    nd = len(shape)
    return pl.BlockSpec(shape, lambda *_: (0,) * nd, pipeline_mode=pl.Buffered(1))


def _layer_spec(stacked, l):
    nd = stacked.ndim
    return pl.BlockSpec((None,) + stacked.shape[1:], lambda *_: (l,) + (0,) * (nd - 1),
                        pipeline_mode=pl.Buffered(1))


def _rms(x, g):
    return x * lax.rsqrt(jnp.mean(x * x, axis=-1, keepdims=True) + EPS) * g


def _lane_ids(shape):
    return lax.broadcasted_iota(jnp.int32, shape, len(shape) - 1)


def _rope_block(x, cos, sin_signed, first, shift):
    up = pltpu.roll(x, LANES - shift, 1)
    dn = pltpu.roll(x, shift, 1)
    return x * cos + jnp.where(first, up, dn) * sin_signed


def _rope(x, cos, sin_signed, first, shift):
    blocks = [
        _rope_block(x[:, i * LANES:(i + 1) * LANES], cos, sin_signed, first, shift)
        for i in range(x.shape[1] // LANES)
    ]
    return blocks[0] if len(blocks) == 1 else jnp.concatenate(blocks, axis=1)


def _group_sum(x, gmat):
    hi = x.astype(BF16)
    lo = (x - hi.astype(F32)).astype(BF16)
    return (jnp.dot(hi, gmat, preferred_element_type=F32)
            + jnp.dot(lo, gmat, preferred_element_type=F32))


def _store_groups(ref, rows, x):
    for g in range(x.shape[1] // LANES):
        ref[0, g, rows, :] = x[:, g * LANES:(g + 1) * LANES].astype(ref.dtype)


def _store_groups_t(ref, rows, x):
    for g in range(x.shape[1] // LANES):
        ref[0, g, :, rows] = x[:, g * LANES:(g + 1) * LANES].T.astype(ref.dtype)


def _store_values(ref, rows, v):
    ones = jnp.ones((v.shape[0], LANES), ref.dtype)
    for g in range(v.shape[1] // LANES):
        ref[0, g, rows, :] = jnp.concatenate(
            [v[:, g * LANES:(g + 1) * LANES].astype(ref.dtype), ones], axis=1)


def _proj_kernel(x_ref, g1_ref, w_ref, aqg_ref, akg_ref, cqg_ref, ckvg_ref,
                 wuq_ref, wkn_ref, wv_ref, gmat_ref,
                 cosa_ref, sina_ref, cosb_ref, sinb_ref, cosc_ref, sinc_ref,
                 qa_ref, ka_ref, va_ref, qb_ref, kb_ref, vb_ref,
                 qc_ref, kc_ref, vc_ref):
    lane = _lane_ids((1, LANES))
    first_a = (lane % 32) < 16
    first_b = (lane % 32) < 4
    first_c = (lane >= 64) & (lane < 80)
    gmat = gmat_ref[...]

    def head_norm(t, g, gm):
        ms = _group_sum(t * t, gm) * (1.0 / A_HEAD_DIM)
        return t * lax.rsqrt(ms + EPS) * g

    for sub in range(x_ref.shape[0] // SUB_PROJ):
        rows = pl.ds(sub * SUB_PROJ, SUB_PROJ)
        h = _rms(x_ref[rows, :], g1_ref[...]).astype(BF16)
        proj = jnp.dot(h, w_ref[...], preferred_element_type=F32)

        def p(off, width):
            return proj[:, off:off + width]

        cosa, sina = cosa_ref[rows, :], sina_ref[rows, :]
        cosb, sinb = cosb_ref[rows, :], sinb_ref[rows, :]
        cosc, sinc = cosc_ref[rows, :], sinc_ref[rows, :]

        cq = _rms(p(OFF_CQ, C_Q_RANK), cqg_ref[...]).astype(BF16)
        q_up = jnp.dot(cq, wuq_ref[...], preferred_element_type=F32)
        _store_groups(qc_ref, rows, _rope(q_up, cosc, sinc, first_c, 16)
                      * ((C_NOPE_DIM + C_ROPE_DIM) ** -0.5 * LOG2E))
        ckv = _rms(p(OFF_CKV, C_KV_RANK), ckvg_ref[...]).astype(BF16)
        k_nope = jnp.dot(ckv, wkn_ref[...], preferred_element_type=F32)
        k_rope = _rope(p(OFF_CKR, LANES), cosc, sinc, first_c, 16)
        _store_groups_t(kc_ref, rows,
                        k_nope + jnp.concatenate([k_rope] * C_HEADS, axis=1))
        _store_values(vc_ref, rows, jnp.dot(ckv, wv_ref[...], preferred_element_type=F32))

        aq = head_norm(p(OFF_AQ, 512), aqg_ref[...], gmat)
        _store_groups(qa_ref, rows, _rope(aq, cosa, sina, first_a, 16)
                      * (A_HEAD_DIM ** -0.5 * LOG2E))
        ak = head_norm(p(OFF_AK, 128), akg_ref[...], gmat[:LANES, :LANES])
        _store_groups_t(ka_ref, rows, _rope(ak, cosa, sina, first_a, 16))

        _store_groups(qb_ref, rows, _rope(p(OFF_BQ, 512), cosb, sinb, first_b, 4)
                      * (B_QK_DIM ** -0.5 * LOG2E))
        _store_groups_t(kb_ref, rows,
                        _rope(p(OFF_BK, 512), cosb, sinb, first_b, 4))

        _store_values(va_ref, rows, p(OFF_AV, 128))
        _store_values(vb_ref, rows, p(OFF_BV, 512))


def _proj_call(x2, l, layered, gmat, tables, batch, seq):
    n = x2.shape[0]
    tm = min(TM_PROJ, seq)
    nt = seq // tm
    tab_spec = pl.BlockSpec((tm, LANES), lambda i: (i % nt, 0))

    def rows(g, w):
        return (jax.ShapeDtypeStruct((batch, g, seq, w), BF16),
                pl.BlockSpec((1, g, tm, w), lambda i: (i // nt, 0, i % nt, 0)))

    def cols(g):
        return (jax.ShapeDtypeStruct((batch, g, LANES, seq), BF16),
                pl.BlockSpec((1, g, LANES, tm), lambda i: (i // nt, 0, 0, i % nt)))

    outs = [rows(4, LANES), cols(1), rows(1, 2 * LANES),
            rows(4, LANES), cols(4), rows(4, 2 * LANES),
            rows(8, LANES), cols(8), rows(4, 2 * LANES)]
    return pl.pallas_call(
        _proj_kernel,
        grid=(n // tm,),
        in_specs=[pl.BlockSpec((tm, D_MODEL), lambda i: (i, 0))]
        + [_layer_spec(a, l) for a in layered]
        + [_const_spec(gmat.shape)] + [tab_spec] * 6,
        out_specs=[o[1] for o in outs],
        out_shape=[o[0] for o in outs],
        compiler_params=_cparams(1),
        name="proj_prep",
    )(x2, *layered, gmat, *tables)


def _run_pipeline(n_items, stage1, stage2, stage3):
    assert n_items % 2 == 0
    i32 = jnp.int32
    stage1(i32(0), 0)
    stage1(i32(1), 1)
    stage2(0)

    def body(u, carry):
        t = 2 * u + 1
        stage1(t + 1, 0)
        stage2(1)
        stage1(t + 2, 1)
        stage3(t - 1, 0)
        stage2(0)
        stage3(t, 1)
        return carry

    lax.fori_loop(0, (n_items - 2) // 2, body, 0)
    stage2(1)
    stage3(i32(n_items - 2), 0)
    stage3(i32(n_items - 1), 1)


def _item_locator(n_rows, rows_per_item, halves):
    n_chunks = n_rows // rows_per_item
    assert n_chunks & (n_chunks - 1) == 0 and halves in (1, 2)
    shift = n_chunks.bit_length() - 1

    def locate(t):
        u = t >> (halves - 1)
        c = u & (n_chunks - 1)
        return u >> shift, pl.ds(pl.multiple_of(c * rows_per_item, rows_per_item),
                                 rows_per_item)

    return locate, halves * OUT_GROUPS * n_chunks


def _attn_pipeline(rows_per_item, n_sub, halves, get_qk, get_v, finalize, o_ref,
                   s_buf, m_buf, e_buf):
    locate, n_items = _item_locator(o_ref.shape[2], rows_per_item, halves)

    def stage1(t, slot):
        g, rows = locate(t)
        for j in range(n_sub):
            q, kt = get_qk(g, rows, j, slot)
            s = jnp.dot(q, kt, preferred_element_type=F32)
            s_buf[slot, j] = s
            m_buf[slot, j] = jnp.max(s, axis=-1, keepdims=True)

    def stage2(slot):
        for j in range(n_sub):
            e_buf[slot, j] = jnp.exp2(s_buf[slot, j] - m_buf[slot, j]).astype(BF16)

    def stage3(t, slot):
        g, rows = locate(t)
        outs = []
        for j in range(n_sub):
            r = jnp.dot(e_buf[slot, j], get_v(g, j), preferred_element_type=F32)
            outs.append(r[:, :LANES] / r[:, LANES:])
        o = finalize(outs, slot).astype(o_ref.dtype)
        if halves == 1:
            o_ref[0, g, rows, :] = o
        else:
            lo = slot * (LANES // 2)
            o_ref[0, g, rows, lo:lo + LANES // 2] = o[:, lo:lo + LANES // 2]

    _run_pipeline(n_items, stage1, stage2, stage3)


def _attn_a_kernel(q_ref, k_ref, v_ref, o_ref, s_buf, m_buf, e_buf, *, rc):
    lane = _lane_ids((1, LANES))
    low = lane < 64

    def get_qk(g, rows, j, half):
        q = q_ref[0, g, rows, :]
        zero = jnp.zeros_like(q)
        return (jnp.where(low, q, zero) if j == 0 else jnp.where(low, zero, q)), k_ref[0, 0]

    def get_v(g, j):
        return v_ref[0, 0]

    def finalize(outs, half):
        return jnp.where(low, outs[0], outs[1])

    _attn_pipeline(rc, 2, 1, get_qk, get_v, finalize, o_ref, s_buf, m_buf, e_buf)


def _attn_b_kernel(q_ref, k_ref, v_ref, lam_ref, linit_ref, subg_ref, o_ref,
                   s_buf, m_buf, e_buf, *, rc):
    lane = _lane_ids((1, LANES))
    lf = lam_ref[...]
    lam = (jnp.exp(jnp.sum(lf[0:1] * lf[1:2], axis=-1, keepdims=True))
           - jnp.exp(jnp.sum(lf[2:3] * lf[3:4], axis=-1, keepdims=True))
           + linit_ref[:, 0:1])
    post = subg_ref[...] * (1.0 - linit_ref[...])

    def get_qk(g, rows, j, half):
        q = q_ref[0, g, rows, :]
        lo = half * B_V_DIM + j * B_QK_DIM
        sel = (lane >= lo) & (lane < lo + B_QK_DIM)
        return jnp.where(sel, q, jnp.zeros_like(q)), k_ref[0, g]

    def get_v(g, j):
        return v_ref[0, g]

    def finalize(outs, half):
        o = outs[0] - lam * outs[1]
        in_head = (lane >= half * B_V_DIM) & (lane < (half + 1) * B_V_DIM)
        ms = jnp.sum(jnp.where(in_head, o * o, 0.0), axis=-1,
                     keepdims=True) * (1.0 / B_V_DIM)
        return o * lax.rsqrt(ms + EPS) * post

    _attn_pipeline(rc, 2, 2, get_qk, get_v, finalize, o_ref, s_buf, m_buf, e_buf)


def _attn_c_kernel(q_ref, k_ref, v_ref, o_ref, s_buf, m_buf, e_buf, *, rc):
    def get_qk(g, rows, j, half):
        return q_ref[0, 2 * g + half, rows, :], k_ref[0, 2 * g + half]

    def get_v(g, j):
        return v_ref[0, g]

    def finalize(outs, half):
        return outs[0]

    _attn_pipeline(rc, 1, 2, get_qk, get_v, finalize, o_ref, s_buf, m_buf, e_buf)


def _softmax_scratch(n_sub, rc, s):
    return [pltpu.VMEM((2, n_sub, rc, s), F32),
            pltpu.VMEM((2, n_sub, rc, 1), F32),
            pltpu.VMEM((2, n_sub, rc, s), BF16)]


def _attn_call(kernel, name, q, kt, v, l, extra, n_sub, rc):
    b, _, s, _ = q.shape
    rc = min(rc, s)

    def batch_spec(a):
        return pl.BlockSpec((1,) + a.shape[1:], lambda bi: (bi, 0, 0, 0))

    return pl.pallas_call(
        functools.partial(kernel, rc=rc),
        grid=(b,),
        in_specs=[batch_spec(q), batch_spec(kt), batch_spec(v)]
        + [_layer_spec(e, l) for e in extra],
        out_specs=pl.BlockSpec((1, OUT_GROUPS, s, LANES), lambda bi: (bi, 0, 0, 0)),
        out_shape=jax.ShapeDtypeStruct((b, OUT_GROUPS, s, LANES), BF16),
        scratch_shapes=_softmax_scratch(n_sub, rc, s),
        compiler_params=_cparams(1),
        name=name,
    )(q, kt, v, *extra)


def _merge_kernel(x_ref, g1_ref, ya_ref, yb_ref, yc_ref, wg_ref, wb_ref, wo_ref, o_ref):
    x = x_ref[...]
    h = _rms(x, g1_ref[...]).astype(BF16)
    merged = None
    for n, y_ref in enumerate((ya_ref, yb_ref, yc_ref)):
        logits = jnp.dot(h, wg_ref[:, n * D_MODEL:(n + 1) * D_MODEL],
                         preferred_element_type=F32)
        y = jnp.concatenate([y_ref[0, g] for g in range(OUT_GROUPS)], axis=1)
        z = jnp.dot(y, wb_ref[n], preferred_element_type=F32)
        term = jax.nn.sigmoid(logits) * z
        merged = term if merged is None else merged + term
    o_ref[...] = x + jnp.dot(merged.astype(BF16), wo_ref[...],
                             preferred_element_type=F32)


def _merge_call(x2, l, g1, ya, yb, yc, wg, wb, wo):
    n = x2.shape[0]
    tm = min(TM_MERGE, ya.shape[2])
    nt = ya.shape[2] // tm
    tok = pl.BlockSpec((tm, D_MODEL), lambda i: (i, 0))
    y_spec = pl.BlockSpec((1, OUT_GROUPS, tm, LANES), lambda i: (i // nt, 0, i % nt, 0))
    return pl.pallas_call(
        _merge_kernel,
        grid=(n // tm,),
        in_specs=[tok, _layer_spec(g1, l), y_spec, y_spec, y_spec,
                  _layer_spec(wg, l), _layer_spec(wb, l), _layer_spec(wo, l)],
        out_specs=tok,
        out_shape=jax.ShapeDtypeStruct((n, D_MODEL), F32),
        compiler_params=_cparams(1),
        name="gated_merge",
    )(x2, g1, ya, yb, yc, wg, wb, wo)


FF_CHUNK = 1024


def _mlp_kernel(x_ref, g2_ref, w1_ref, w2_ref, gf_ref, o_ref, *, final):
    x = x_ref[...]
    h = _rms(x, g2_ref[...]).astype(BF16)
    acc = x
    for c in range(D_FF // FF_CHUNK):
        cols = pl.ds(c * FF_CHUNK, FF_CHUNK)
        hid = jnp.dot(h, w1_ref[:, cols], preferred_element_type=F32)
        hid = jnp.square(jnp.maximum(hid, 0.0)).astype(BF16)
        acc = acc + jnp.dot(hid, w2_ref[cols, :], preferred_element_type=F32)
    if final:
        acc = _rms(acc, gf_ref[...])
    o_ref[...] = acc


def _mlp_call(x2, l, g2, w1, w2, gf, final):
    n = x2.shape[0]
    tm = TM_MLP
    tok = pl.BlockSpec((tm, D_MODEL), lambda i: (i, 0))
    return pl.pallas_call(
        functools.partial(_mlp_kernel, final=final),
        grid=(n // tm,),
        in_specs=[tok, _layer_spec(g2, l), _layer_spec(w1, l),
                  _layer_spec(w2, l), _const_spec(gf.shape)],
        out_specs=tok,
        out_shape=jax.ShapeDtypeStruct((n, D_MODEL), F32),
        compiler_params=_cparams(1),
        name="mlp_final" if final else "mlp",
    )(x2, g2, w1, w2, gf)


def _angles(pos, dim, theta):
    inv_freq = theta ** (-jnp.arange(0, dim, 2, dtype=F32) / dim)
    return pos.astype(F32)[:, None] * inv_freq[None, :]


def _rope_tables(seq):
    t = jnp.arange(seq)
    half = A_HEAD_DIM // 2
    row_ang = _angles(t // GRID_W, half, AXIAL_THETA)
    col_ang = _angles(t % GRID_W, half, AXIAL_THETA)
    b_ang = _angles(t, B_ROT_DIM, ROPE_THETA)
    c_ang = _angles(t, C_ROPE_DIM, MLA_THETA)

    def pair(ang):
        c, s = jnp.cos(ang), jnp.sin(ang)
        return jnp.concatenate([c, c], 1), jnp.concatenate([-s, s], 1)

    one = lambda w: jnp.ones((seq, w), F32)
    zero = lambda w: jnp.zeros((seq, w), F32)

    rc, rs = pair(row_ang)
    cc, cs = pair(col_ang)
    cos_a = jnp.concatenate([rc, cc] * 2, 1)
    sin_a = jnp.concatenate([rs, cs] * 2, 1)

    bc, bs = pair(b_ang)
    cos_b = jnp.concatenate([bc, one(B_QK_DIM - B_ROT_DIM)] * 4, 1)
    sin_b = jnp.concatenate([bs, zero(B_QK_DIM - B_ROT_DIM)] * 4, 1)

    mc, ms = pair(c_ang)
    cos_c = jnp.concatenate([one(64), mc, one(32)], 1)
    sin_c = jnp.concatenate([zero(64), ms, zero(32)], 1)
    return cos_a, sin_a, cos_b, sin_b, cos_c, sin_c


def _arrange_weights(w_in, c_w_uq, c_w_ukv, w_branch):
    depth = w_in.shape[0]
    order = jnp.array(A_HEAD_ORDER)
    a_q = w_in[:, :, 0:512].reshape(depth, D_MODEL, A_HEADS, A_HEAD_DIM)
    a_q = a_q[:, :, order, :].reshape(depth, D_MODEL, 512)
    c_kr = jnp.zeros((depth, D_MODEL, LANES), w_in.dtype).at[:, :, 64:96].set(
        w_in[:, :, 2944:2976])
    w_main = jnp.concatenate(
        [w_in[:, :, 2304:2944], c_kr, a_q, w_in[:, :, 512:640], w_in[:, :, 768:1792],
         w_in[:, :, 640:768], w_in[:, :, 1792:2304]], axis=2)
    assert w_main.shape[2] == PROJ_COLS
    w_gate = w_in[:, :, 2976:]

    uq = c_w_uq.reshape(depth, C_Q_RANK, C_HEADS, C_NOPE_DIM + C_ROPE_DIM)
    uq = jnp.pad(uq, ((0, 0), (0, 0), (0, 0), (0, LANES - C_NOPE_DIM - C_ROPE_DIM)))
    wuq = uq.reshape(depth, C_Q_RANK, C_HEADS * LANES)
    ukv = c_w_ukv.reshape(depth, C_KV_RANK, C_HEADS, C_NOPE_DIM + C_V_DIM)
    kn = jnp.pad(ukv[..., :C_NOPE_DIM], ((0, 0), (0, 0), (0, 0), (0, LANES - C_NOPE_DIM)))
    wkn = kn.reshape(depth, C_KV_RANK, C_HEADS * LANES)
    wv = ukv[..., C_NOPE_DIM:].reshape(depth, C_KV_RANK, C_HEADS * C_V_DIM)

    wb_a = w_branch[:, 0].reshape(depth, A_HEADS, A_HEAD_DIM, D_MODEL)[:, order]
    wb = jnp.stack([wb_a.reshape(depth, BRANCH_W, D_MODEL),
                    w_branch[:, 1], w_branch[:, 2]], axis=1)
    return w_main, w_gate, wuq, wkn, wv, wb


def kernel(x, ln1_g, w_in, a_q_norm, a_k_norm, b_lambda, b_subln, c_q_norm, c_kv_norm,
           c_w_uq, c_w_ukv, w_branch, w_out, ln2_g, w_ff1, w_ff2, final_g):
    b, s, d = x.shape
    n = b * s
    tables = _rope_tables(s)
    gidx = jnp.arange(BRANCH_W) // A_HEAD_DIM
    gmat = (gidx[:, None] == gidx[None, :]).astype(BF16)
    gf = final_g.reshape(1, d)
    w_in, c_w_uq, c_w_ukv, w_branch, w_out, w_ff1, w_ff2 = (
        t.astype(BF16) for t in (w_in, c_w_uq, c_w_ukv, w_branch, w_out, w_ff1, w_ff2))

    depth = w_in.shape[0]
    w_main, w_gate, wuq, wkn, wv, wb = _arrange_weights(w_in, c_w_uq, c_w_ukv, w_branch)
    row = lambda t: t.reshape(depth, 1, t.shape[-1])
    g1, g2 = row(ln1_g), row(ln2_g)
    proj_params = (g1, w_main, row(jnp.tile(a_q_norm, (1, A_HEADS))),
                   row(jnp.tile(a_k_norm, (1, A_KV_HEADS))), row(c_q_norm), row(c_kv_norm),
                   wuq, wkn, wv)
    lambda_init = jnp.array([0.8 - 0.6 * math.exp(-0.3 * l) for l in range(depth)], F32)
    diff_params = (b_lambda, jnp.broadcast_to(lambda_init[:, None, None], (depth, 1, LANES)),
                   row(jnp.tile(b_subln, (1, 2))))

    x2 = x.reshape(n, d)
    for l in range(depth):
        qa, ka, va, qb, kb, vb, qc, kc, vc = _proj_call(x2, l, proj_params, gmat, tables, b, s)
        ya = _attn_call(_attn_a_kernel, "attn_gqa", qa, ka, va, l, (), 2, RC_GQA)
        yb = _attn_call(_attn_b_kernel, "attn_diff", qb, kb, vb, l, diff_params, 2, RC_DIFF)
        yc = _attn_call(_attn_c_kernel, "attn_mla", qc, kc, vc, l, (), 1, RC_MLA)
        x2 = _merge_call(x2, l, g1, ya, yb, yc, w_gate, wb, w_out)
        x2 = _mlp_call(x2, l, g2, w_ff1, w_ff2, gf, l == depth - 1)
    return x2.reshape(b, s, d)
```

```python
import functools
import math

import jax
import jax.numpy as jnp
import numpy as np
from jax import lax
from jax.experimental import pallas as pl
from jax.experimental.pallas import tpu as pltpu

F32 = jnp.float32
BF16 = jnp.bfloat16

D_MODEL = 1024
DEPTH = 4
GRID_W = 64
EPS = 1e-6

A_HEADS = 8
A_KV_HEADS = 2
A_HEAD_DIM = 64
AXIAL_THETA = 10000.0

B_HEADS = 8
B_QK_DIM = 32
B_V_DIM = 64
B_ROT_DIM = 8
ROPE_THETA = 500000.0

C_HEADS = 8
C_Q_RANK = 384
C_KV_RANK = 256
C_NOPE_DIM = 64
C_ROPE_DIM = 32
C_V_DIM = 64
MLA_THETA = 10000.0

N_BRANCH = 3
BRANCH_W = 512
D_FF = 4 * D_MODEL

LANES = 128
LOG2E = math.log2(math.e)
VMEM_LIMIT = 52 * 1024 * 1024
OUT_GROUPS = BRANCH_W // LANES

OFF_CQ, OFF_CKV, OFF_CKR = 0, 384, 640
OFF_AQ, OFF_AK = 768, 1280
OFF_BQ, OFF_BK = 1408, 1920
OFF_AV, OFF_BV = 2432, 2560
PROJ_COLS = 3072

TM_PROJ = 1024
SUB_PROJ = 512
TM_MERGE = 1024
TM_MLP = 1024
RC_GQA = 512
RC_DIFF = 512
RC_MLA = 1024

A_HEAD_ORDER = (0, 4, 1, 5, 2, 6, 3, 7)


def _cparams(n_axes):
    return pltpu.CompilerParams(
        dimension_semantics=("parallel",) * n_axes,
        vmem_limit_bytes=VMEM_LIMIT,
    )


def _const_spec(shape):
    nd = len(shape)
    return pl.BlockSpec(shape, lambda *_: (0,) * nd, pipeline_mode=pl.Buffered(1))


def _layer_spec(stacked, l):
    nd = stacked.ndim
    return pl.BlockSpec((None,) + stacked.shape[1:], lambda *_: (l,) + (0,) * (nd - 1),
                        pipeline_mode=pl.Buffered(1))


def _rms(x, g):
    return x * lax.rsqrt(jnp.mean(x * x, axis=-1, keepdims=True) + EPS) * g


def _lane_ids(shape):
    return lax.broadcasted_iota(jnp.int32, shape, len(shape) - 1)


def _rope(x, cos, sin_signed):
    blocks = []
    for i in range(x.shape[1] // LANES):
        xb = x[:, i * LANES:(i + 1) * LANES]
        blocks.append(xb * cos + pltpu.roll(xb, LANES // 2, 1) * sin_signed)
    return blocks[0] if len(blocks) == 1 else jnp.concatenate(blocks, axis=1)


def _group_sum(x, gmat):
    hi = x.astype(BF16)
    lo = (x - hi.astype(F32)).astype(BF16)
    return (jnp.dot(hi, gmat, preferred_element_type=F32)
            + jnp.dot(lo, gmat, preferred_element_type=F32))


def _store_groups(ref, rows, x):
    for g in range(x.shape[1] // LANES):
        ref[0, g, rows, :] = x[:, g * LANES:(g + 1) * LANES].astype(ref.dtype)


def _store_groups_t(ref, rows, x):
    for g in range(x.shape[1] // LANES):
        ref[0, g, :, rows] = x[:, g * LANES:(g + 1) * LANES].T.astype(ref.dtype)


def _store_values(ref, rows, v):
    ones = jnp.ones((v.shape[0], LANES), ref.dtype)
    for g in range(v.shape[1] // LANES):
        ref[0, g, rows, :] = jnp.concatenate(
            [v[:, g * LANES:(g + 1) * LANES].astype(ref.dtype), ones], axis=1)


def _proj_kernel(x_ref, g1_ref, w_ref, aqg_ref, akg_ref, cqg_ref, ckvg_ref,
                 wuq_ref, wkn_ref, wv_ref, gmat_ref,
                 cosa_ref, sina_ref, cosb_ref, sinb_ref, cosc_ref, sinc_ref,
                 qa_ref, ka_ref, va_ref, qb_ref, kb_ref, vb_ref,
                 qc_ref, kc_ref, vc_ref):
    gmat = gmat_ref[...]

    def head_norm(t, g, gm):
        ms = _group_sum(t * t, gm) * (1.0 / A_HEAD_DIM)
        return t * lax.rsqrt(ms + EPS) * g

    for sub in range(x_ref.shape[0] // SUB_PROJ):
        rows = pl.ds(sub * SUB_PROJ, SUB_PROJ)
        h = _rms(x_ref[rows, :], g1_ref[...]).astype(BF16)
        proj = jnp.dot(h, w_ref[...], preferred_element_type=F32)

        def p(off, width):
            return proj[:, off:off + width]

        cosa, sina = cosa_ref[rows, :], sina_ref[rows, :]
        cosb, sinb = cosb_ref[rows, :], sinb_ref[rows, :]
        cosc, sinc = cosc_ref[rows, :], sinc_ref[rows, :]

        cq = _rms(p(OFF_CQ, C_Q_RANK), cqg_ref[...]).astype(BF16)
        q_up = jnp.dot(cq, wuq_ref[...], preferred_element_type=F32)
        _store_groups(qc_ref, rows, _rope(q_up, cosc, sinc)
                      * ((C_NOPE_DIM + C_ROPE_DIM) ** -0.5 * LOG2E))
        ckv = _rms(p(OFF_CKV, C_KV_RANK), ckvg_ref[...]).astype(BF16)
        k_nope = jnp.dot(ckv, wkn_ref[...], preferred_element_type=F32)
        k_rope = _rope(p(OFF_CKR, LANES), cosc, sinc)
        _store_groups_t(kc_ref, rows,
                        k_nope + jnp.concatenate([k_rope] * C_HEADS, axis=1))
        _store_values(vc_ref, rows, jnp.dot(ckv, wv_ref[...], preferred_element_type=F32))

        aq = head_norm(p(OFF_AQ, 512), aqg_ref[...], gmat)
        _store_groups(qa_ref, rows, _rope(aq, cosa, sina) * (A_HEAD_DIM ** -0.5 * LOG2E))
        ak = head_norm(p(OFF_AK, 128), akg_ref[...], gmat[:LANES, :LANES])
        _store_groups_t(ka_ref, rows, _rope(ak, cosa, sina))

        _store_groups(qb_ref, rows, _rope(p(OFF_BQ, 512), cosb, sinb)
                      * (B_QK_DIM ** -0.5 * LOG2E))
        _store_groups_t(kb_ref, rows, _rope(p(OFF_BK, 512), cosb, sinb))

        _store_values(va_ref, rows, p(OFF_AV, 128))
        _store_values(vb_ref, rows, p(OFF_BV, 512))


def _proj_call(x2, l, layered, gmat, tables, batch, seq):
    n = x2.shape[0]
    tm = min(TM_PROJ, seq)
    nt = seq // tm
    tab_spec = pl.BlockSpec((tm, LANES), lambda i: (i % nt, 0))

    def rows(g, w):
        return (jax.ShapeDtypeStruct((batch, g, seq, w), BF16),
                pl.BlockSpec((1, g, tm, w), lambda i: (i // nt, 0, i % nt, 0)))

    def cols(g):
        return (jax.ShapeDtypeStruct((batch, g, LANES, seq), BF16),
                pl.BlockSpec((1, g, LANES, tm), lambda i: (i // nt, 0, 0, i % nt)))

    outs = [rows(4, LANES), cols(1), rows(1, 2 * LANES),
            rows(4, LANES), cols(4), rows(4, 2 * LANES),
            rows(8, LANES), cols(8), rows(4, 2 * LANES)]
    return pl.pallas_call(
        _proj_kernel,
        grid=(n // tm,),
        in_specs=[pl.BlockSpec((tm, D_MODEL), lambda i: (i, 0))]
        + [_layer_spec(a, l) for a in layered]
        + [_const_spec(gmat.shape)] + [tab_spec] * 6,
        out_specs=[o[1] for o in outs],
        out_shape=[o[0] for o in outs],
        compiler_params=_cparams(1),
        name="proj_prep",
    )(x2, *layered, gmat, *tables)


def _run_pipeline(n_items, stage1, stage2, stage3):
    assert n_items % 2 == 0
    i32 = jnp.int32
    stage1(i32(0), 0)
    stage1(i32(1), 1)
    stage2(0)

    def body(u, carry):
        t = 2 * u + 1
        stage1(t + 1, 0)
        stage2(1)
        stage1(t + 2, 1)
        stage3(t - 1, 0)
        stage2(0)
        stage3(t, 1)
        return carry

    lax.fori_loop(0, (n_items - 2) // 2, body, 0)
    stage2(1)
    stage3(i32(n_items - 2), 0)
    stage3(i32(n_items - 1), 1)


def _item_locator(n_groups, n_rows, rows_per_item, halves):
    n_chunks = n_rows // rows_per_item
    assert n_chunks & (n_chunks - 1) == 0 and halves in (1, 2)
    shift = n_chunks.bit_length() - 1

    def locate(t):
        u = t >> (halves - 1)
        c = u & (n_chunks - 1)
        return u >> shift, pl.ds(pl.multiple_of(c * rows_per_item, rows_per_item),
                                 rows_per_item)

    return locate, halves * n_groups * n_chunks


def _attn_pipeline(rows_per_item, n_sub, halves, get_qk, get_v, finalize, o_ref,
                   s_buf, m_buf, e_buf):
    locate, n_items = _item_locator(o_ref.shape[1], o_ref.shape[2], rows_per_item, halves)

    def stage1(t, slot):
        g, rows = locate(t)
        for j in range(n_sub):
            q, kt = get_qk(g, rows, j, slot)
            s = jnp.dot(q, kt, preferred_element_type=F32)
            s_buf[slot, j] = s
            m_buf[slot, j] = jnp.max(s, axis=-1, keepdims=True)

    def stage2(slot):
        for j in range(n_sub):
            e_buf[slot, j] = jnp.exp2(s_buf[slot, j] - m_buf[slot, j]).astype(BF16)

    def stage3(t, slot):
        g, rows = locate(t)
        outs = []
        for j in range(n_sub):
            r = jnp.dot(e_buf[slot, j], get_v(g, j), preferred_element_type=F32)
            outs.append(r[:, :LANES] / r[:, LANES:])
        o = finalize(outs, slot).astype(o_ref.dtype)
        if halves == 1:
            o_ref[0, g, rows, :] = o
        else:
            lo = slot * (LANES // 2)
            o_ref[0, g, rows, lo:lo + LANES // 2] = o[:, lo:lo + LANES // 2]

    _run_pipeline(n_items, stage1, stage2, stage3)


def _attn_a_kernel(q_ref, k_ref, v_ref, o_ref, *scratch, rc):
    lane = _lane_ids((1, LANES))
    low = lane < 64
    first_head = (lane % 64) < 32

    def get_qk(g, rows, j, half):
        q = q_ref[0, g, rows, :]
        zero = jnp.zeros_like(q)
        return (jnp.where(first_head, q, zero) if j == 0
                else jnp.where(first_head, zero, q)), k_ref[0, 0]

    def get_v(g, j):
        return v_ref[0, 0]

    def finalize(outs, half):
        return jnp.where(low, outs[0], outs[1])

    _attn_pipeline(rc, 2, 1, get_qk, get_v, finalize, o_ref, *scratch)


def _attn_b_kernel(q_ref, k_ref, v_ref, lam_ref, linit_ref, subg_ref, o_ref,
                   *scratch, rc):
    lane = _lane_ids((1, LANES))
    lf = lam_ref[...]
    lam = (jnp.exp(jnp.sum(lf[0:1] * lf[1:2], axis=-1, keepdims=True))
           - jnp.exp(jnp.sum(lf[2:3] * lf[3:4], axis=-1, keepdims=True))
           + linit_ref[:, 0:1])
    post = subg_ref[...] * (1.0 - linit_ref[...])

    def get_qk(g, rows, j, half):
        q = q_ref[0, g, rows, :]
        sel = ((lane % 64) // 16) == 2 * half + j
        return jnp.where(sel, q, jnp.zeros_like(q)), k_ref[0, g]

    def get_v(g, j):
        return v_ref[0, g]

    def finalize(outs, half):
        o = outs[0] - lam * outs[1]
        in_head = (lane >= half * B_V_DIM) & (lane < (half + 1) * B_V_DIM)
        ms = jnp.sum(jnp.where(in_head, o * o, 0.0), axis=-1,
                     keepdims=True) * (1.0 / B_V_DIM)
        return o * lax.rsqrt(ms + EPS) * post

    _attn_pipeline(rc, 2, 2, get_qk, get_v, finalize, o_ref, *scratch)


def _attn_c_kernel(q_ref, k_ref, v_ref, o_ref, *scratch, rc):
    def get_qk(g, rows, j, half):
        return q_ref[0, 2 * g + half, rows, :], k_ref[0, 2 * g + half]

    def get_v(g, j):
        return v_ref[0, g]

    def finalize(outs, half):
        return outs[0]

    _attn_pipeline(rc, 1, 2, get_qk, get_v, finalize, o_ref, *scratch)


def _softmax_scratch(n_sub, rc, s):
    return [pltpu.VMEM((2, n_sub, rc, s), F32),
            pltpu.VMEM((2, n_sub, rc, 1), F32),
            pltpu.VMEM((2, n_sub, rc, s), BF16)]


def _attn_call(kernel, name, q, kt, v, l, extra, n_sub, rc, splits=1):
    b, _, s, _ = q.shape
    rc = min(rc, s)

    def part_spec(a):
        return pl.BlockSpec((1, a.shape[1] // splits) + a.shape[2:],
                            lambda bi, part: (bi, part, 0, 0))

    return pl.pallas_call(
        functools.partial(kernel, rc=rc),
        grid=(b, splits),
        in_specs=[part_spec(q), part_spec(kt), part_spec(v)]
        + [_layer_spec(e, l) for e in extra],
        out_specs=pl.BlockSpec((1, OUT_GROUPS // splits, s, LANES),
                               lambda bi, part: (bi, part, 0, 0)),
        out_shape=jax.ShapeDtypeStruct((b, OUT_GROUPS, s, LANES), BF16),
        scratch_shapes=_softmax_scratch(n_sub, rc, s),
        compiler_params=_cparams(2),
        name=name,
    )(q, kt, v, *extra)


def _merge_kernel(x_ref, g1_ref, ya_ref, yb_ref, yc_ref, wg_ref, wb_ref, wo_ref, o_ref):
    x = x_ref[...]
    h = _rms(x, g1_ref[...]).astype(BF16)
    merged = None
    for n, y_ref in enumerate((ya_ref, yb_ref, yc_ref)):
        logits = jnp.dot(h, wg_ref[:, n * D_MODEL:(n + 1) * D_MODEL],
                         preferred_element_type=F32)
        y = jnp.concatenate([y_ref[0, g] for g in range(OUT_GROUPS)], axis=1)
        z = jnp.dot(y, wb_ref[n], preferred_element_type=F32)
        term = jax.nn.sigmoid(logits) * z
        merged = term if merged is None else merged + term
    o_ref[...] = x + jnp.dot(merged.astype(BF16), wo_ref[...],
                             preferred_element_type=F32)


def _merge_call(x2, l, g1, ya, yb, yc, wg, wb, wo):
    n = x2.shape[0]
    tm = min(TM_MERGE, ya.shape[2])
    nt = ya.shape[2] // tm
    tok = pl.BlockSpec((tm, D_MODEL), lambda i: (i, 0))
    y_spec = pl.BlockSpec((1, OUT_GROUPS, tm, LANES), lambda i: (i // nt, 0, i % nt, 0))
    return pl.pallas_call(
        _merge_kernel,
        grid=(n // tm,),
        in_specs=[tok, _layer_spec(g1, l), y_spec, y_spec, y_spec,
                  _layer_spec(wg, l), _layer_spec(wb, l), _layer_spec(wo, l)],
        out_specs=tok,
        out_shape=jax.ShapeDtypeStruct((n, D_MODEL), F32),
        compiler_params=_cparams(1),
        name="gated_merge",
    )(x2, g1, ya, yb, yc, wg, wb, wo)


FF_CHUNK = 2048


def _mlp_kernel(x_ref, g2_ref, w1_ref, w2_ref, gf_ref, o_ref, *, final):
    x = x_ref[...]
    h = _rms(x, g2_ref[...]).astype(BF16)
    acc = x
    for c in range(D_FF // FF_CHUNK):
        cols = pl.ds(c * FF_CHUNK, FF_CHUNK)
        hid = jnp.dot(h, w1_ref[:, cols], preferred_element_type=F32)
        hid = jnp.square(jnp.maximum(hid, 0.0)).astype(BF16)
        acc = acc + jnp.dot(hid, w2_ref[cols, :], preferred_element_type=F32)
    if final:
        acc = _rms(acc, gf_ref[...])
    o_ref[...] = acc


def _mlp_call(x2, l, g2, w1, w2, gf, final):
    n = x2.shape[0]
    tm = TM_MLP
    tok = pl.BlockSpec((tm, D_MODEL), lambda i: (i, 0))
    return pl.pallas_call(
        functools.partial(_mlp_kernel, final=final),
        grid=(n // tm,),
        in_specs=[tok, _layer_spec(g2, l), _layer_spec(w1, l),
                  _layer_spec(w2, l), _const_spec(gf.shape)],
        out_specs=tok,
        out_shape=jax.ShapeDtypeStruct((n, D_MODEL), F32),
        compiler_params=_cparams(1),
        name="mlp_final" if final else "mlp",
    )(x2, g2, w1, w2, gf)


def _angles(pos, dim, theta):
    inv_freq = theta ** (-jnp.arange(0, dim, 2, dtype=F32) / dim)
    return pos.astype(F32)[:, None] * inv_freq[None, :]


def _rope_lane_perms():
    n = np.arange(LANES)
    half, rest = n // 64, n % 64
    perm_a = 64 * (rest // 32) + 32 * ((rest % 32) // 16) + 16 * half + rest % 16
    comp, u = rest // 16, rest % 16
    perm_b = np.where(u < 4, 32 * comp + 4 * half + u, 32 * comp + 8 + 12 * half + (u - 4))
    perm_c = np.concatenate([np.arange(64, 80), np.arange(0, 48), np.arange(80, 96),
                             np.arange(48, 64), np.arange(96, 128)])
    return perm_a, perm_b, perm_c


def _permute_blocks(t, perm):
    blocks = t.reshape(t.shape[:-1] + (t.shape[-1] // LANES, LANES))
    return blocks[..., perm].reshape(t.shape)


def _rope_tables(seq):
    t = jnp.arange(seq)
    half = A_HEAD_DIM // 2
    row_ang = _angles(t // GRID_W, half, AXIAL_THETA)
    col_ang = _angles(t % GRID_W, half, AXIAL_THETA)
    b_ang = _angles(t, B_ROT_DIM, ROPE_THETA)
    c_ang = _angles(t, C_ROPE_DIM, MLA_THETA)

    def pair(ang):
        c, s = jnp.cos(ang), jnp.sin(ang)
        return jnp.concatenate([c, c], 1), jnp.concatenate([-s, s], 1)

    one = lambda w: jnp.ones((seq, w), F32)
    zero = lambda w: jnp.zeros((seq, w), F32)

    rc, rs = pair(row_ang)
    cc, cs = pair(col_ang)
    cos_a = jnp.concatenate([rc, cc] * 2, 1)
    sin_a = jnp.concatenate([rs, cs] * 2, 1)

    bc, bs = pair(b_ang)
    cos_b = jnp.concatenate([bc, one(B_QK_DIM - B_ROT_DIM)] * 4, 1)
    sin_b = jnp.concatenate([bs, zero(B_QK_DIM - B_ROT_DIM)] * 4, 1)

    mc, ms = pair(c_ang)
    cos_c = jnp.concatenate([one(64), mc, one(32)], 1)
    sin_c = jnp.concatenate([zero(64), ms, zero(32)], 1)
    perm_a, perm_b, perm_c = _rope_lane_perms()
    return (cos_a[:, perm_a], sin_a[:, perm_a], cos_b[:, perm_b], sin_b[:, perm_b],
            cos_c[:, perm_c], sin_c[:, perm_c])


def _arrange_weights(w_in, c_w_uq, c_w_ukv, w_branch):
    depth = w_in.shape[0]
    order = jnp.array(A_HEAD_ORDER)
    perm_a, perm_b, perm_c = _rope_lane_perms()
    a_q = w_in[:, :, 0:512].reshape(depth, D_MODEL, A_HEADS, A_HEAD_DIM)
    a_q = _permute_blocks(a_q[:, :, order, :].reshape(depth, D_MODEL, 512), perm_a)
    a_k = _permute_blocks(w_in[:, :, 512:640], perm_a)
    b_qk = _permute_blocks(w_in[:, :, 768:1792], perm_b)
    c_kr = jnp.zeros((depth, D_MODEL, LANES), w_in.dtype).at[:, :, 64:96].set(
        w_in[:, :, 2944:2976])
    w_main = jnp.concatenate(
        [w_in[:, :, 2304:2944], c_kr[..., perm_c], a_q, a_k, b_qk,
         w_in[:, :, 640:768], w_in[:, :, 1792:2304]], axis=2)
    assert w_main.shape[2] == PROJ_COLS
    w_gate = w_in[:, :, 2976:]

    uq = c_w_uq.reshape(depth, C_Q_RANK, C_HEADS, C_NOPE_DIM + C_ROPE_DIM)
    uq = jnp.pad(uq, ((0, 0), (0, 0), (0, 0), (0, LANES - C_NOPE_DIM - C_ROPE_DIM)))
    wuq = uq[..., perm_c].reshape(depth, C_Q_RANK, C_HEADS * LANES)
    ukv = c_w_ukv.reshape(depth, C_KV_RANK, C_HEADS, C_NOPE_DIM + C_V_DIM)
    kn = jnp.pad(ukv[..., :C_NOPE_DIM], ((0, 0), (0, 0), (0, 0), (0, LANES - C_NOPE_DIM)))
    wkn = kn[..., perm_c].reshape(depth, C_KV_RANK, C_HEADS * LANES)
    wv = ukv[..., C_NOPE_DIM:].reshape(depth, C_KV_RANK, C_HEADS * C_V_DIM)

    wb_a = w_branch[:, 0].reshape(depth, A_HEADS, A_HEAD_DIM, D_MODEL)[:, order]
    wb = jnp.stack([wb_a.reshape(depth, BRANCH_W, D_MODEL),
                    w_branch[:, 1], w_branch[:, 2]], axis=1)
    return w_main, w_gate, wuq, wkn, wv, wb


def kernel(x, ln1_g, w_in, a_q_norm, a_k_norm, b_lambda, b_subln, c_q_norm, c_kv_norm,
           c_w_uq, c_w_ukv, w_branch, w_out, ln2_g, w_ff1, w_ff2, final_g):
    b, s, d = x.shape
    n = b * s
    tables = _rope_tables(s)
    perm_a = _rope_lane_perms()[0]
    gidx = _permute_blocks(jnp.arange(BRANCH_W) // A_HEAD_DIM, perm_a)
    gmat = (gidx[:, None] == gidx[None, :]).astype(BF16)
    gf = final_g.reshape(1, d)
    w_in, c_w_uq, c_w_ukv, w_branch, w_out, w_ff1, w_ff2 = (
        t.astype(BF16) for t in (w_in, c_w_uq, c_w_ukv, w_branch, w_out, w_ff1, w_ff2))

    depth = w_in.shape[0]
    w_main, w_gate, wuq, wkn, wv, wb = _arrange_weights(w_in, c_w_uq, c_w_ukv, w_branch)
    row = lambda t: t.reshape(depth, 1, t.shape[-1])
    g1, g2 = row(ln1_g), row(ln2_g)
    proj_params = (g1, w_main,
                   row(_permute_blocks(jnp.tile(a_q_norm, (1, A_HEADS)), perm_a)),
                   row(_permute_blocks(jnp.tile(a_k_norm, (1, A_KV_HEADS)), perm_a)),
                   row(c_q_norm), row(c_kv_norm),
                   wuq, wkn, wv)
    lambda_init = jnp.array([0.8 - 0.6 * math.exp(-0.3 * l) for l in range(depth)], F32)
    diff_params = (b_lambda, jnp.broadcast_to(lambda_init[:, None, None], (depth, 1, LANES)),
                   row(jnp.tile(b_subln, (1, 2))))

    x2 = x.reshape(n, d)
    for l in range(depth):
        qa, ka, va, qb, kb, vb, qc, kc, vc = _proj_call(x2, l, proj_params, gmat, tables, b, s)
        ya = _attn_call(_attn_a_kernel, "attn_gqa", qa, ka, va, l, (), 2, RC_GQA)
        yb = _attn_call(_attn_b_kernel, "attn_diff", qb, kb, vb, l, diff_params, 2, RC_DIFF)
        yc = _attn_call(_attn_c_kernel, "attn_mla", qc, kc, vc, l, (), 1, RC_MLA, splits=2)
        x2 = _merge_call(x2, l, g1, ya, yb, yc, w_gate, wb, w_out)
        x2 = _mlp_call(x2, l, g2, w_ff1, w_ff2, gf, l == depth - 1)
    return x2.reshape(b, s, d)
```

```python
import functools
import math

import jax
import jax.numpy as jnp
import numpy as np
from jax import lax
from jax.experimental import pallas as pl
from jax.experimental.pallas import tpu as pltpu

F32 = jnp.float32
BF16 = jnp.bfloat16

D_MODEL = 1024
DEPTH = 4
GRID_W = 64
EPS = 1e-6

A_HEADS = 8
A_KV_HEADS = 2
A_HEAD_DIM = 64
AXIAL_THETA = 10000.0

B_HEADS = 8
B_QK_DIM = 32
B_V_DIM = 64
B_ROT_DIM = 8
ROPE_THETA = 500000.0

C_HEADS = 8
C_Q_RANK = 384
C_KV_RANK = 256
C_NOPE_DIM = 64
C_ROPE_DIM = 32
C_V_DIM = 64
MLA_THETA = 10000.0

N_BRANCH = 3
BRANCH_W = 512
D_FF = 4 * D_MODEL

LANES = 128
LOG2E = math.log2(math.e)
VMEM_LIMIT = 52 * 1024 * 1024
OUT_GROUPS = BRANCH_W // LANES

OFF_CQ, OFF_CKV, OFF_CKR = 0, 384, 640
OFF_AQ, OFF_AK = 768, 1280
OFF_BQ, OFF_BK = 1408, 1920
OFF_AV, OFF_BV = 2432, 2560
PROJ_COLS = 3072

TM_PROJ = 1024
SUB_PROJ = 512
TM_MERGE = 1024
TM_MLP = 1024
RC_GQA = 512
RC_DIFF = 512
RC_MLA = 1024

A_HEAD_ORDER = (0, 4, 1, 5, 2, 6, 3, 7)


def _cparams(n_axes):
    return pltpu.CompilerParams(
        dimension_semantics=("parallel",) * n_axes,
        vmem_limit_bytes=VMEM_LIMIT,
    )


def _const_spec(shape):
    nd = len(shape)
    return pl.BlockSpec(shape, lambda *_: (0,) * nd, pipeline_mode=pl.Buffered(1))


def _layer_spec(stacked, l):
    nd = stacked.ndim
    return pl.BlockSpec((None,) + stacked.shape[1:], lambda *_: (l,) + (0,) * (nd - 1),
                        pipeline_mode=pl.Buffered(1))


def _rms(x, g):
    return x * lax.rsqrt(jnp.mean(x * x, axis=-1, keepdims=True) + EPS) * g


def _lane_ids(shape):
    return lax.broadcasted_iota(jnp.int32, shape, len(shape) - 1)


def _rope(x, cos, sin_signed):
    blocks = []
    for i in range(x.shape[1] // LANES):
        xb = x[:, i * LANES:(i + 1) * LANES]
        blocks.append(xb * cos + pltpu.roll(xb, LANES // 2, 1) * sin_signed)
    return blocks[0] if len(blocks) == 1 else jnp.concatenate(blocks, axis=1)


def _group_sum(x, gmat):
    hi = x.astype(BF16)
    lo = (x - hi.astype(F32)).astype(BF16)
    return (jnp.dot(hi, gmat, preferred_element_type=F32)
            + jnp.dot(lo, gmat, preferred_element_type=F32))


def _store_groups(ref, rows, x):
    for g in range(x.shape[1] // LANES):
        ref[0, g, rows, :] = x[:, g * LANES:(g + 1) * LANES].astype(ref.dtype)


def _store_groups_t(ref, rows, x):
    for g in range(x.shape[1] // LANES):
        ref[0, g, :, rows] = x[:, g * LANES:(g + 1) * LANES].T.astype(ref.dtype)


def _store_values(ref, rows, v):
    ones = jnp.ones((v.shape[0], LANES), ref.dtype)
    for g in range(v.shape[1] // LANES):
        ref[0, g, rows, :] = jnp.concatenate(
            [v[:, g * LANES:(g + 1) * LANES].astype(ref.dtype), ones], axis=1)


def _proj_kernel(x_ref, g1_ref, w_ref, aqg_ref, akg_ref, cqg_ref, ckvg_ref,
                 wuq_ref, wkn_ref, wv_ref, gmat_ref,
                 cosa_ref, sina_ref, cosb_ref, sinb_ref, cosc_ref, sinc_ref,
                 qa_ref, ka_ref, va_ref, qb_ref, kb_ref, vb_ref,
                 qc_ref, kc_ref, vc_ref):
    gmat = gmat_ref[...]

    def head_norm(t, g, gm):
        ms = _group_sum(t * t, gm) * (1.0 / A_HEAD_DIM)
        return t * lax.rsqrt(ms + EPS) * g

    for sub in range(x_ref.shape[0] // SUB_PROJ):
        rows = pl.ds(sub * SUB_PROJ, SUB_PROJ)
        h = _rms(x_ref[rows, :], g1_ref[...]).astype(BF16)
        proj = jnp.dot(h, w_ref[...], preferred_element_type=F32)

        def p(off, width):
            return proj[:, off:off + width]

        cosa, sina = cosa_ref[rows, :], sina_ref[rows, :]
        cosb, sinb = cosb_ref[rows, :], sinb_ref[rows, :]
        cosc, sinc = cosc_ref[rows, :], sinc_ref[rows, :]

        cq = _rms(p(OFF_CQ, C_Q_RANK), cqg_ref[...]).astype(BF16)
        q_up = jnp.dot(cq, wuq_ref[...], preferred_element_type=F32)
        _store_groups(qc_ref, rows, _rope(q_up, cosc, sinc)
                      * ((C_NOPE_DIM + C_ROPE_DIM) ** -0.5 * LOG2E))
        ckv = _rms(p(OFF_CKV, C_KV_RANK), ckvg_ref[...]).astype(BF16)
        k_nope = jnp.dot(ckv, wkn_ref[...], preferred_element_type=F32)
        k_rope = _rope(p(OFF_CKR, LANES), cosc, sinc)
        _store_groups_t(kc_ref, rows,
                        k_nope + jnp.concatenate([k_rope] * C_HEADS, axis=1))
        _store_values(vc_ref, rows, jnp.dot(ckv, wv_ref[...], preferred_element_type=F32))

        aq = head_norm(p(OFF_AQ, 512), aqg_ref[...], gmat)
        _store_groups(qa_ref, rows, _rope(aq, cosa, sina) * (A_HEAD_DIM ** -0.5 * LOG2E))
        ak = head_norm(p(OFF_AK, 128), akg_ref[...], gmat[:LANES, :LANES])
        _store_groups_t(ka_ref, rows, _rope(ak, cosa, sina))

        _store_groups(qb_ref, rows, _rope(p(OFF_BQ, 512), cosb, sinb)
                      * (B_QK_DIM ** -0.5 * LOG2E))
        _store_groups_t(kb_ref, rows, _rope(p(OFF_BK, 512), cosb, sinb))

        _store_values(va_ref, rows, p(OFF_AV, 128))
        _store_values(vb_ref, rows, p(OFF_BV, 512))


def _proj_call(x2, l, layered, gmat, tables, batch, seq):
    n = x2.shape[0]
    tm = min(TM_PROJ, seq)
    nt = seq // tm
    tab_spec = pl.BlockSpec((tm, LANES), lambda i: (i % nt, 0))

    def rows(g, w):
        return (jax.ShapeDtypeStruct((batch, g, seq, w), BF16),
                pl.BlockSpec((1, g, tm, w), lambda i: (i // nt, 0, i % nt, 0)))

    def cols(g):
        return (jax.ShapeDtypeStruct((batch, g, LANES, seq), BF16),
                pl.BlockSpec((1, g, LANES, tm), lambda i: (i // nt, 0, 0, i % nt)))

    outs = [rows(4, LANES), cols(1), rows(1, 2 * LANES),
            rows(4, LANES), cols(4), rows(4, 2 * LANES),
            rows(8, LANES), cols(8), rows(4, 2 * LANES)]
    return pl.pallas_call(
        _proj_kernel,
        grid=(n // tm,),
        in_specs=[pl.BlockSpec((tm, D_MODEL), lambda i: (i, 0))]
        + [_layer_spec(a, l) for a in layered]
        + [_const_spec(gmat.shape)] + [tab_spec] * 6,
        out_specs=[o[1] for o in outs],
        out_shape=[o[0] for o in outs],
        compiler_params=_cparams(1),
        name="proj_prep",
    )(x2, *layered, gmat, *tables)


def _run_pipeline(n_items, stage1, stage2, stage3):
    assert n_items % 2 == 0
    i32 = jnp.int32
    stage1(i32(0), 0)
    stage1(i32(1), 1)
    stage2(0)

    def body(u, carry):
        t = 2 * u + 1
        stage1(t + 1, 0)
        stage2(1)
        stage1(t + 2, 1)
        stage3(t - 1, 0)
        stage2(0)
        stage3(t, 1)
        return carry

    lax.fori_loop(0, (n_items - 2) // 2, body, 0)
    stage2(1)
    stage3(i32(n_items - 2), 0)
    stage3(i32(n_items - 1), 1)


def _item_locator(n_groups, n_rows, rows_per_item, halves):
    n_chunks = n_rows // rows_per_item
    assert n_chunks & (n_chunks - 1) == 0 and halves in (1, 2)
    shift = n_chunks.bit_length() - 1

    def locate(t):
        u = t >> (halves - 1)
        c = u & (n_chunks - 1)
        return u >> shift, pl.ds(pl.multiple_of(c * rows_per_item, rows_per_item),
                                 rows_per_item)

    return locate, halves * n_groups * n_chunks


def _attn_pipeline(rows_per_item, n_sub, halves, get_qk, get_v, finalize, o_ref,
                   s_buf, m_buf, e_buf):
    locate, n_items = _item_locator(o_ref.shape[1], o_ref.shape[2], rows_per_item, halves)

    def stage1(t, slot):
        g, rows = locate(t)
        for j in range(n_sub):
            q, kt = get_qk(g, rows, j, slot)
            s = jnp.dot(q, kt, preferred_element_type=F32)
            s_buf[slot, j] = s
            m_buf[slot, j] = jnp.max(s, axis=-1, keepdims=True)

    def stage2(slot):
        for j in range(n_sub):
            e_buf[slot, j] = jnp.exp2(s_buf[slot, j] - m_buf[slot, j]).astype(BF16)

    def stage3(t, slot):
        g, rows = locate(t)
        outs = []
        for j in range(n_sub):
            r = jnp.dot(e_buf[slot, j], get_v(g, j), preferred_element_type=F32)
            outs.append(r[:, :LANES] / r[:, LANES:])
        o = finalize(outs, slot).astype(o_ref.dtype)
        if halves == 1:
            o_ref[0, g, rows, :] = o
        else:
            lo = slot * (LANES // 2)
            o_ref[0, g, rows, lo:lo + LANES // 2] = o[:, lo:lo + LANES // 2]

    _run_pipeline(n_items, stage1, stage2, stage3)


def _attn_a_kernel(q_ref, k_ref, v_ref, o_ref, *scratch, rc):
    lane = _lane_ids((1, LANES))
    low = lane < 64
    first_head = (lane % 64) < 32

    def get_qk(g, rows, j, half):
        q = q_ref[0, g, rows, :]
        zero = jnp.zeros_like(q)
        return (jnp.where(first_head, q, zero) if j == 0
                else jnp.where(first_head, zero, q)), k_ref[0, 0]

    def get_v(g, j):
        return v_ref[0, 0]

    def finalize(outs, half):
        return jnp.where(low, outs[0], outs[1])

    _attn_pipeline(rc, 2, 1, get_qk, get_v, finalize, o_ref, *scratch)


def _attn_b_kernel(q_ref, k_ref, v_ref, lam_ref, linit_ref, subg_ref, o_ref,
                   *scratch, rc):
    lane = _lane_ids((1, LANES))
    lf = lam_ref[...]
    lam = (jnp.exp(jnp.sum(lf[0:1] * lf[1:2], axis=-1, keepdims=True))
           - jnp.exp(jnp.sum(lf[2:3] * lf[3:4], axis=-1, keepdims=True))
           + linit_ref[:, 0:1])
    post = subg_ref[...] * (1.0 - linit_ref[...])

    def get_qk(g, rows, j, half):
        q = q_ref[0, g, rows, :]
        sel = ((lane % 64) // 16) == 2 * half + j
        return jnp.where(sel, q, jnp.zeros_like(q)), k_ref[0, g]

    def get_v(g, j):
        return v_ref[0, g]

    def finalize(outs, half):
        o = outs[0] - lam * outs[1]
        in_head = (lane >= half * B_V_DIM) & (lane < (half + 1) * B_V_DIM)
        ms = jnp.sum(jnp.where(in_head, o * o, 0.0), axis=-1,
                     keepdims=True) * (1.0 / B_V_DIM)
        return o * lax.rsqrt(ms + EPS) * post

    _attn_pipeline(rc, 2, 2, get_qk, get_v, finalize, o_ref, *scratch)


def _attn_c_kernel(q_ref, k_ref, v_ref, o_ref, *scratch, rc):
    def get_qk(g, rows, j, half):
        return q_ref[0, 2 * g + half, rows, :], k_ref[0, 2 * g + half]

    def get_v(g, j):
        return v_ref[0, g]

    def finalize(outs, half):
        return outs[0]

    _attn_pipeline(rc, 1, 2, get_qk, get_v, finalize, o_ref, *scratch)


def _softmax_scratch(n_sub, rc, s):
    return [pltpu.VMEM((2, n_sub, rc, s), F32),
            pltpu.VMEM((2, n_sub, rc, 1), F32),
            pltpu.VMEM((2, n_sub, rc, s), BF16)]


def _attn_call(kernel, name, q, kt, v, l, extra, n_sub, rc, splits=1):
    b, _, s, _ = q.shape
    rc = min(rc, s)

    def part_spec(a):
        return pl.BlockSpec((1, a.shape[1] // splits) + a.shape[2:],
                            lambda bi, part: (bi, part, 0, 0))

    return pl.pallas_call(
        functools.partial(kernel, rc=rc),
        grid=(b, splits),
        in_specs=[part_spec(q), part_spec(kt), part_spec(v)]
        + [_layer_spec(e, l) for e in extra],
        out_specs=pl.BlockSpec((1, OUT_GROUPS // splits, s, LANES),
                               lambda bi, part: (bi, part, 0, 0)),
        out_shape=jax.ShapeDtypeStruct((b, OUT_GROUPS, s, LANES), BF16),
        scratch_shapes=_softmax_scratch(n_sub, rc, s),
        compiler_params=_cparams(2),
        name=name,
    )(q, kt, v, *extra)


def _merge_kernel(x_ref, g1_ref, ya_ref, yb_ref, yc_ref, wg_ref, wb_ref, wo_ref, o_ref):
    x = x_ref[...]
    h = _rms(x, g1_ref[...]).astype(BF16)
    merged = None
    for n, y_ref in enumerate((ya_ref, yb_ref, yc_ref)):
        logits = jnp.dot(h, wg_ref[:, n * D_MODEL:(n + 1) * D_MODEL],
                         preferred_element_type=F32)
        y = jnp.concatenate([y_ref[0, g] for g in range(OUT_GROUPS)], axis=1)
        z = jnp.dot(y, wb_ref[n], preferred_element_type=F32)
        term = jax.nn.sigmoid(logits) * z
        merged = term if merged is None else merged + term
    o_ref[...] = x + jnp.dot(merged.astype(BF16), wo_ref[...],
                             preferred_element_type=F32)


def _merge_call(x2, l, g1, ya, yb, yc, wg, wb, wo):
    n = x2.shape[0]
    tm = min(TM_MERGE, ya.shape[2])
    nt = ya.shape[2] // tm
    tok = pl.BlockSpec((tm, D_MODEL), lambda i: (i, 0))
    y_spec = pl.BlockSpec((1, OUT_GROUPS, tm, LANES), lambda i: (i // nt, 0, i % nt, 0))
    return pl.pallas_call(
        _merge_kernel,
        grid=(n // tm,),
        in_specs=[tok, _layer_spec(g1, l), y_spec, y_spec, y_spec,
                  _layer_spec(wg, l), _layer_spec(wb, l), _layer_spec(wo, l)],
        out_specs=tok,
        out_shape=jax.ShapeDtypeStruct((n, D_MODEL), F32),
        compiler_params=_cparams(1),
        name="gated_merge",
    )(x2, g1, ya, yb, yc, wg, wb, wo)


FF_CHUNK = 2048


def _mlp_kernel(x_ref, g2_ref, w1_ref, w2_ref, gf_ref, o_ref, *, final):
    x = x_ref[...]
    h = _rms(x, g2_ref[...]).astype(BF16)
    acc = x
    for c in range(D_FF // FF_CHUNK):
        cols = pl.ds(c * FF_CHUNK, FF_CHUNK)
        hid = jnp.dot(h, w1_ref[:, cols], preferred_element_type=F32)
        hid = jnp.square(jnp.maximum(hid, 0.0)).astype(BF16)
        acc = acc + jnp.dot(hid, w2_ref[cols, :], preferred_element_type=F32)
    if final:
        acc = _rms(acc, gf_ref[...])
    o_ref[...] = acc


def _mlp_call(x2, l, g2, w1, w2, gf, final):
    n = x2.shape[0]
    tm = TM_MLP
    tok = pl.BlockSpec((tm, D_MODEL), lambda i: (i, 0))
    return pl.pallas_call(
        functools.partial(_mlp_kernel, final=final),
        grid=(n // tm,),
        in_specs=[tok, _layer_spec(g2, l), _layer_spec(w1, l),
                  _layer_spec(w2, l), _const_spec(gf.shape)],
        out_specs=tok,
        out_shape=jax.ShapeDtypeStruct((n, D_MODEL), F32),
        compiler_params=_cparams(1),
        name="mlp_final" if final else "mlp",
    )(x2, g2, w1, w2, gf)


def _angles(pos, dim, theta):
    inv_freq = theta ** (-jnp.arange(0, dim, 2, dtype=F32) / dim)
    return pos.astype(F32)[:, None] * inv_freq[None, :]


def _rope_lane_perms():
    n = np.arange(LANES)
    half, rest = n // 64, n % 64
    perm_a = 64 * (rest // 32) + 32 * ((rest % 32) // 16) + 16 * half + rest % 16
    comp, u = rest // 16, rest % 16
    perm_b = np.where(u < 4, 32 * comp + 4 * half + u, 32 * comp + 8 + 12 * half + (u - 4))
    perm_c = np.concatenate([np.arange(64, 80), np.arange(0, 48), np.arange(80, 96),
                             np.arange(48, 64), np.arange(96, 128)])
    return perm_a, perm_b, perm_c


def _permute_blocks(t, perm):
    blocks = t.reshape(t.shape[:-1] + (t.shape[-1] // LANES, LANES))
    return blocks[..., perm].reshape(t.shape)


def _to_rope_layout_a(t):
    lead = t.shape[:-1]
    blocks = t.reshape(lead + (t.shape[-1] // LANES, 2, 2, 2, 16))
    k = len(lead)
    return blocks.transpose(tuple(range(k + 1)) + (k + 3, k + 1, k + 2, k + 4)).reshape(t.shape)


def _to_rope_layout_b(t):
    lead = t.shape[:-1]
    k = len(lead)
    blocks = t.reshape(lead + (t.shape[-1] // LANES, 4, 32))
    swap = tuple(range(k + 1)) + (k + 2, k + 1, k + 3)
    rot = blocks[..., :8].reshape(lead + (-1, 4, 2, 4)).transpose(swap)
    rest = blocks[..., 8:].reshape(lead + (-1, 4, 2, 12)).transpose(swap)
    return jnp.concatenate([rot, rest], axis=-1).reshape(t.shape)


def _to_rope_layout_c(t):
    blocks = t.reshape(t.shape[:-1] + (t.shape[-1] // LANES, LANES))
    out = jnp.concatenate([blocks[..., 64:80], blocks[..., 0:48], blocks[..., 80:96],
                           blocks[..., 48:64], blocks[..., 96:128]], axis=-1)
    return out.reshape(t.shape)


def _rope_tables(seq):
    t = jnp.arange(seq)
    half = A_HEAD_DIM // 2
    row_ang = _angles(t // GRID_W, half, AXIAL_THETA)
    col_ang = _angles(t % GRID_W, half, AXIAL_THETA)
    b_ang = _angles(t, B_ROT_DIM, ROPE_THETA)
    c_ang = _angles(t, C_ROPE_DIM, MLA_THETA)

    def pair(ang):
        c, s = jnp.cos(ang), jnp.sin(ang)
        return jnp.concatenate([c, c], 1), jnp.concatenate([-s, s], 1)

    one = lambda w: jnp.ones((seq, w), F32)
    zero = lambda w: jnp.zeros((seq, w), F32)

    rc, rs = pair(row_ang)
    cc, cs = pair(col_ang)
    cos_a = jnp.concatenate([rc, cc] * 2, 1)
    sin_a = jnp.concatenate([rs, cs] * 2, 1)

    bc, bs = pair(b_ang)
    cos_b = jnp.concatenate([bc, one(B_QK_DIM - B_ROT_DIM)] * 4, 1)
    sin_b = jnp.concatenate([bs, zero(B_QK_DIM - B_ROT_DIM)] * 4, 1)

    mc, ms = pair(c_ang)
    cos_c = jnp.concatenate([one(64), mc, one(32)], 1)
    sin_c = jnp.concatenate([zero(64), ms, zero(32)], 1)
    return (_to_rope_layout_a(cos_a), _to_rope_layout_a(sin_a),
            _to_rope_layout_b(cos_b), _to_rope_layout_b(sin_b),
            _to_rope_layout_c(cos_c), _to_rope_layout_c(sin_c))


def _arrange_weights(w_in, c_w_uq, c_w_ukv, w_branch):
    depth = w_in.shape[0]
    order = jnp.array(A_HEAD_ORDER)
    a_q = w_in[:, :, 0:512].reshape(depth, D_MODEL, A_HEADS, A_HEAD_DIM)
    a_q = _to_rope_layout_a(a_q[:, :, order, :].reshape(depth, D_MODEL, 512))
    a_k = _to_rope_layout_a(w_in[:, :, 512:640])
    b_qk = _to_rope_layout_b(w_in[:, :, 768:1792])
    c_kr = jnp.zeros((depth, D_MODEL, LANES), w_in.dtype).at[:, :, 64:96].set(
        w_in[:, :, 2944:2976])
    w_main = jnp.concatenate(
        [w_in[:, :, 2304:2944], _to_rope_layout_c(c_kr), a_q, a_k, b_qk,
         w_in[:, :, 640:768], w_in[:, :, 1792:2304]], axis=2)
    assert w_main.shape[2] == PROJ_COLS
    w_gate = w_in[:, :, 2976:]

    uq = c_w_uq.reshape(depth, C_Q_RANK, C_HEADS, C_NOPE_DIM + C_ROPE_DIM)
    uq = jnp.pad(uq, ((0, 0), (0, 0), (0, 0), (0, LANES - C_NOPE_DIM - C_ROPE_DIM)))
    wuq = _to_rope_layout_c(uq.reshape(depth, C_Q_RANK, C_HEADS * LANES))
    ukv = c_w_ukv.reshape(depth, C_KV_RANK, C_HEADS, C_NOPE_DIM + C_V_DIM)
    kn = jnp.pad(ukv[..., :C_NOPE_DIM], ((0, 0), (0, 0), (0, 0), (0, LANES - C_NOPE_DIM)))
    wkn = _to_rope_layout_c(kn.reshape(depth, C_KV_RANK, C_HEADS * LANES))
    wv = ukv[..., C_NOPE_DIM:].reshape(depth, C_KV_RANK, C_HEADS * C_V_DIM)

    wb_a = w_branch[:, 0].reshape(depth, A_HEADS, A_HEAD_DIM, D_MODEL)[:, order]
    wb = jnp.stack([wb_a.reshape(depth, BRANCH_W, D_MODEL),
                    w_branch[:, 1], w_branch[:, 2]], axis=1)
    return w_main, w_gate, wuq, wkn, wv, wb


def kernel(x, ln1_g, w_in, a_q_norm, a_k_norm, b_lambda, b_subln, c_q_norm, c_kv_norm,
           c_w_uq, c_w_ukv, w_branch, w_out, ln2_g, w_ff1, w_ff2, final_g):
    b, s, d = x.shape
    n = b * s
    tables = _rope_tables(s)
    perm_a = _rope_lane_perms()[0]
    gidx = _permute_blocks(jnp.arange(BRANCH_W) // A_HEAD_DIM, perm_a)
    gmat = (gidx[:, None] == gidx[None, :]).astype(BF16)
    gf = final_g.reshape(1, d)
    w_in, c_w_uq, c_w_ukv, w_branch, w_out, w_ff1, w_ff2 = (
        t.astype(BF16) for t in (w_in, c_w_uq, c_w_ukv, w_branch, w_out, w_ff1, w_ff2))

    depth = w_in.shape[0]
    w_main, w_gate, wuq, wkn, wv, wb = _arrange_weights(w_in, c_w_uq, c_w_ukv, w_branch)
    row = lambda t: t.reshape(depth, 1, t.shape[-1])
    g1, g2 = row(ln1_g), row(ln2_g)
    proj_params = (g1, w_main,
                   row(_to_rope_layout_a(jnp.tile(a_q_norm, (1, A_HEADS)))),
                   row(_to_rope_layout_a(jnp.tile(a_k_norm, (1, A_KV_HEADS)))),
                   row(c_q_norm), row(c_kv_norm),
                   wuq, wkn, wv)
    lambda_init = jnp.array([0.8 - 0.6 * math.exp(-0.3 * l) for l in range(depth)], F32)
    diff_params = (b_lambda, jnp.broadcast_to(lambda_init[:, None, None], (depth, 1, LANES)),
                   row(jnp.tile(b_subln, (1, 2))))

    x2 = x.reshape(n, d)
    for l in range(depth):
        qa, ka, va, qb, kb, vb, qc, kc, vc = _proj_call(x2, l, proj_params, gmat, tables, b, s)
        ya = _attn_call(_attn_a_kernel, "attn_gqa", qa, ka, va, l, (), 2, RC_GQA)
        yb = _attn_call(_attn_b_kernel, "attn_diff", qb, kb, vb, l, diff_params, 2, RC_DIFF)
        yc = _attn_call(_attn_c_kernel, "attn_mla", qc, kc, vc, l, (), 1, RC_MLA, splits=2)
        x2 = _merge_call(x2, l, g1, ya, yb, yc, w_gate, wb, w_out)
        x2 = _mlp_call(x2, l, g2, w_ff1, w_ff2, gf, l == depth - 1)
    return x2.reshape(b, s, d)
```

```python
import functools
import math

import jax
import jax.numpy as jnp
from jax import lax
from jax.experimental import pallas as pl
from jax.experimental.pallas import tpu as pltpu

F32 = jnp.float32
BF16 = jnp.bfloat16

D_MODEL = 1024
DEPTH = 4
GRID_W = 64
EPS = 1e-6

A_HEADS = 8
A_KV_HEADS = 2
A_HEAD_DIM = 64
AXIAL_THETA = 10000.0

B_HEADS = 8
B_QK_DIM = 32
B_V_DIM = 64
B_ROT_DIM = 8
ROPE_THETA = 500000.0

C_HEADS = 8
C_Q_RANK = 384
C_KV_RANK = 256
C_NOPE_DIM = 64
C_ROPE_DIM = 32
C_V_DIM = 64
MLA_THETA = 10000.0

N_BRANCH = 3
BRANCH_W = 512
D_FF = 4 * D_MODEL

LANES = 128
LOG2E = math.log2(math.e)
VMEM_LIMIT = 52 * 1024 * 1024
OUT_GROUPS = BRANCH_W // LANES

OFF_CQ, OFF_CKV, OFF_CKR = 0, 384, 640
OFF_AQ, OFF_AK = 768, 1280
OFF_BQ, OFF_BK = 1408, 1920
OFF_AV, OFF_BV = 2432, 2560
PROJ_COLS = 3072

TM_PROJ = 1024
SUB_PROJ = 512
TM_MERGE = 1024
TM_MLP = 1024
RC_GQA = 1024
RC_DIFF = 512
RC_MLA = 1024

A_HEAD_ORDER = (0, 4, 1, 5, 2, 6, 3, 7)


def _cparams(n_axes):
    return pltpu.CompilerParams(
        dimension_semantics=("parallel",) * n_axes,
        vmem_limit_bytes=VMEM_LIMIT,
    )


def _const_spec(shape):
    nd = len(shape)
    return pl.BlockSpec(shape, lambda *_: (0,) * nd, pipeline_mode=pl.Buffered(1))


def _layer_spec(stacked, l):
    nd = stacked.ndim
    return pl.BlockSpec((None,) + stacked.shape[1:], lambda *_: (l,) + (0,) * (nd - 1),
                        pipeline_mode=pl.Buffered(1))


def _rms(x, g):
    return x * lax.rsqrt(jnp.mean(x * x, axis=-1, keepdims=True) + EPS) * g


def _lane_ids(shape):
    return lax.broadcasted_iota(jnp.int32, shape, len(shape) - 1)


def _rope_block(x, cos, sin_signed, first, shift):
    up = pltpu.roll(x, LANES - shift, 1)
    dn = pltpu.roll(x, shift, 1)
    return x * cos + jnp.where(first, up, dn) * sin_signed


def _rope(x, cos, sin_signed, first, shift):
    blocks = [
        _rope_block(x[:, i * LANES:(i + 1) * LANES], cos, sin_signed, first, shift)
        for i in range(x.shape[1] // LANES)
    ]
    return blocks[0] if len(blocks) == 1 else jnp.concatenate(blocks, axis=1)


def _group_sum(x, gmat):
    hi = x.astype(BF16)
    lo = (x - hi.astype(F32)).astype(BF16)
    return (jnp.dot(hi, gmat, preferred_element_type=F32)
            + jnp.dot(lo, gmat, preferred_element_type=F32))


def _store_groups(ref, rows, x):
    for g in range(x.shape[1] // LANES):
        ref[0, g, rows, :] = x[:, g * LANES:(g + 1) * LANES].astype(ref.dtype)


def _store_groups_t(ref, rows, x):
    for g in range(x.shape[1] // LANES):
        ref[0, g, :, rows] = x[:, g * LANES:(g + 1) * LANES].T.astype(ref.dtype)


def _store_values(ref, rows, v):
    ones = jnp.ones((v.shape[0], LANES), ref.dtype)
    for g in range(v.shape[1] // LANES):
        ref[0, g, rows, :] = jnp.concatenate(
            [v[:, g * LANES:(g + 1) * LANES].astype(ref.dtype), ones], axis=1)


def _proj_kernel(x_ref, g1_ref, w_ref, aqg_ref, akg_ref, cqg_ref, ckvg_ref,
                 wuq_ref, wkn_ref, wv_ref, gmat_ref,
                 cosa_ref, sina_ref, cosb_ref, sinb_ref, cosc_ref, sinc_ref,
                 qa_ref, ka_ref, va_ref, qb_ref, kb_ref, vb_ref,
                 qc_ref, kc_ref, vc_ref):
    lane = _lane_ids((1, LANES))
    first_a = (lane % 32) < 16
    first_b = (lane % 32) < 4
    first_c = (lane >= 64) & (lane < 80)
    gmat = gmat_ref[...]

    def head_norm(t, g, gm):
        ms = _group_sum(t * t, gm) * (1.0 / A_HEAD_DIM)
        return t * lax.rsqrt(ms + EPS) * g

    for sub in range(x_ref.shape[0] // SUB_PROJ):
        rows = pl.ds(sub * SUB_PROJ, SUB_PROJ)
        h = _rms(x_ref[rows, :], g1_ref[...]).astype(BF16)
        proj = jnp.dot(h, w_ref[...], preferred_element_type=F32)

        def p(off, width):
            return proj[:, off:off + width]

        cosa, sina = cosa_ref[rows, :], sina_ref[rows, :]
        cosb, sinb = cosb_ref[rows, :], sinb_ref[rows, :]
        cosc, sinc = cosc_ref[rows, :], sinc_ref[rows, :]

        cq = _rms(p(OFF_CQ, C_Q_RANK), cqg_ref[...]).astype(BF16)
        q_up = jnp.dot(cq, wuq_ref[...], preferred_element_type=F32)
        _store_groups(qc_ref, rows, _rope(q_up, cosc, sinc, first_c, 16)
                      * ((C_NOPE_DIM + C_ROPE_DIM) ** -0.5 * LOG2E))
        ckv = _rms(p(OFF_CKV, C_KV_RANK), ckvg_ref[...]).astype(BF16)
        k_nope = jnp.dot(ckv, wkn_ref[...], preferred_element_type=F32)
        k_rope = _rope(p(OFF_CKR, LANES), cosc, sinc, first_c, 16)
        _store_groups_t(kc_ref, rows,
                        k_nope + jnp.concatenate([k_rope] * C_HEADS, axis=1))
        _store_values(vc_ref, rows, jnp.dot(ckv, wv_ref[...], preferred_element_type=F32))

        aq = head_norm(p(OFF_AQ, 512), aqg_ref[...], gmat)
        _store_groups(qa_ref, rows, _rope(aq, cosa, sina, first_a, 16)
                      * (A_HEAD_DIM ** -0.5 * LOG2E))
        ak = head_norm(p(OFF_AK, 128), akg_ref[...], gmat[:LANES, :LANES])
        _store_groups_t(ka_ref, rows, _rope(ak, cosa, sina, first_a, 16))

        _store_groups(qb_ref, rows, _rope(p(OFF_BQ, 512), cosb, sinb, first_b, 4)
                      * (B_QK_DIM ** -0.5 * LOG2E))
        _store_groups_t(kb_ref, rows,
                        _rope(p(OFF_BK, 512), cosb, sinb, first_b, 4))

        _store_values(va_ref, rows, p(OFF_AV, 128))
        _store_values(vb_ref, rows, p(OFF_BV, 512))


def _proj_call(x2, l, layered, gmat, tables, batch, seq):
    n = x2.shape[0]
    tm = min(TM_PROJ, seq)
    nt = seq // tm
    tab_spec = pl.BlockSpec((tm, LANES), lambda i: (i % nt, 0))

    def rows(g, w):
        return (jax.ShapeDtypeStruct((batch, g, seq, w), BF16),
                pl.BlockSpec((1, g, tm, w), lambda i: (i // nt, 0, i % nt, 0)))

    def cols(g):
        return (jax.ShapeDtypeStruct((batch, g, LANES, seq), BF16),
                pl.BlockSpec((1, g, LANES, tm), lambda i: (i // nt, 0, 0, i % nt)))

    outs = [rows(4, LANES), cols(1), rows(1, 2 * LANES),
            rows(4, LANES), cols(4), rows(4, 2 * LANES),
            rows(8, LANES), cols(8), rows(4, 2 * LANES)]
    return pl.pallas_call(
        _proj_kernel,
        grid=(n // tm,),
        in_specs=[pl.BlockSpec((tm, D_MODEL), lambda i: (i, 0))]
        + [_layer_spec(a, l) for a in layered]
        + [_const_spec(gmat.shape)] + [tab_spec] * 6,
        out_specs=[o[1] for o in outs],
        out_shape=[o[0] for o in outs],
        compiler_params=_cparams(1),
        name="proj_prep",
    )(x2, *layered, gmat, *tables)


def _run_pipeline(n_items, stage1, stage2, stage3):
    assert n_items % 2 == 0
    i32 = jnp.int32
    stage1(i32(0), 0)
    stage1(i32(1), 1)
    stage2(0)

    def body(u, carry):
        t = 2 * u + 1
        stage1(t + 1, 0)
        stage2(1)
        stage1(t + 2, 1)
        stage3(t - 1, 0)
        stage2(0)
        stage3(t, 1)
        return carry

    lax.fori_loop(0, (n_items - 2) // 2, body, 0)
    stage2(1)
    stage3(i32(n_items - 2), 0)
    stage3(i32(n_items - 1), 1)


def _item_locator(n_groups, n_rows, rows_per_item, halves):
    n_chunks = n_rows // rows_per_item
    assert n_chunks & (n_chunks - 1) == 0 and halves in (1, 2)
    shift = n_chunks.bit_length() - 1

    def locate(t):
        u = t >> (halves - 1)
        c = u & (n_chunks - 1)
        return u >> shift, pl.ds(pl.multiple_of(c * rows_per_item, rows_per_item),
                                 rows_per_item)

    return locate, halves * n_groups * n_chunks


def _attn_pipeline(rows_per_item, n_sub, halves, get_qk, get_v, finalize, o_ref,
                   s_buf, m_buf, e_buf):
    locate, n_items = _item_locator(o_ref.shape[1], o_ref.shape[2], rows_per_item, halves)

    def stage1(t, slot):
        g, rows = locate(t)
        for j in range(n_sub):
            q, kt = get_qk(g, rows, j, slot)
            s = jnp.dot(q, kt, preferred_element_type=F32)
            s_buf[slot, j] = s
            m_buf[slot, j] = jnp.max(s, axis=-1, keepdims=True)

    def stage2(slot):
        for j in range(n_sub):
            e_buf[slot, j] = jnp.exp2(s_buf[slot, j] - m_buf[slot, j]).astype(BF16)

    def stage3(t, slot):
        g, rows = locate(t)
        outs = []
        for j in range(n_sub):
            r = jnp.dot(e_buf[slot, j], get_v(g, j), preferred_element_type=F32)
            outs.append(r[:, :LANES] / r[:, LANES:])
        o = finalize(outs, slot).astype(o_ref.dtype)
        if halves == 1:
            o_ref[0, g, rows, :] = o
        else:
            lo = slot * (LANES // 2)
            o_ref[0, g, rows, lo:lo + LANES // 2] = o[:, lo:lo + LANES // 2]

    _run_pipeline(n_items, stage1, stage2, stage3)


def _attn_a_kernel(q_ref, k_ref, v_ref, o_ref, s_buf, m_buf, e_buf, m2_buf, rb_buf, mb_buf,
                   *, rc):
    lane = _lane_ids((1, LANES))
    low = lane < 64
    kb = k_ref.shape[3] // 2
    locate, n_items = _item_locator(o_ref.shape[1], o_ref.shape[2], rc, 2)

    def stage1(t, slot):
        g, rows = locate(t)
        q = q_ref[0, g, rows, :]
        zero = jnp.zeros_like(q)
        kt = k_ref[0, 0, :, slot * kb:(slot + 1) * kb]
        for j in range(2):
            qj = jnp.where(low, q, zero) if j == 0 else jnp.where(low, zero, q)
            s = jnp.dot(qj, kt, preferred_element_type=F32)
            s_buf[slot, j] = s
            m_buf[slot, j] = jnp.max(s, axis=-1, keepdims=True)

    def stage2(slot):
        for j in range(2):
            m = m_buf[slot, j]
            m2_buf[slot, j] = m
            e_buf[slot, j] = jnp.exp2(s_buf[slot, j] - m).astype(BF16)

    def stage3(t, slot):
        g, rows = locate(t)
        v = v_ref[0, 0, slot * kb:(slot + 1) * kb, :]
        outs = []
        for j in range(2):
            r = jnp.dot(e_buf[slot, j], v, preferred_element_type=F32)
            m = m2_buf[slot, j]
            if slot == 0:
                rb_buf[j] = r
                mb_buf[j] = m
            else:
                m0 = mb_buf[j]
                big = jnp.maximum(m, m0)
                tot = r * jnp.exp2(m - big) + rb_buf[j] * jnp.exp2(m0 - big)
                outs.append(tot[:, :LANES] / tot[:, LANES:])
        if slot == 1:
            o_ref[0, g, rows, :] = jnp.where(low, outs[0], outs[1]).astype(o_ref.dtype)

    _run_pipeline(n_items, stage1, stage2, stage3)


def _attn_b_kernel(q_ref, k_ref, v_ref, lam_ref, linit_ref, subg_ref, o_ref,
                   s_buf, m_buf, e_buf, *, rc):
    lane = _lane_ids((1, LANES))
    lf = lam_ref[...]
    lam = (jnp.exp(jnp.sum(lf[0:1] * lf[1:2], axis=-1, keepdims=True))
           - jnp.exp(jnp.sum(lf[2:3] * lf[3:4], axis=-1, keepdims=True))
           + linit_ref[:, 0:1])
    post = subg_ref[...] * (1.0 - linit_ref[...])

    def get_qk(g, rows, j, half):
        q = q_ref[0, g, rows, :]
        lo = half * B_V_DIM + j * B_QK_DIM
        sel = (lane >= lo) & (lane < lo + B_QK_DIM)
        return jnp.where(sel, q, jnp.zeros_like(q)), k_ref[0, g]

    def get_v(g, j):
        return v_ref[0, g]

    def finalize(outs, half):
        o = outs[0] - lam * outs[1]
        in_head = (lane >= half * B_V_DIM) & (lane < (half + 1) * B_V_DIM)
        ms = jnp.sum(jnp.where(in_head, o * o, 0.0), axis=-1,
                     keepdims=True) * (1.0 / B_V_DIM)
        return o * lax.rsqrt(ms + EPS) * post

    _attn_pipeline(rc, 2, 2, get_qk, get_v, finalize, o_ref, s_buf, m_buf, e_buf)


def _attn_c_kernel(q_ref, k_ref, v_ref, o_ref, s_buf, m_buf, e_buf, *, rc):
    def get_qk(g, rows, j, half):
        return q_ref[0, 2 * g + half, rows, :], k_ref[0, 2 * g + half]

    def get_v(g, j):
        return v_ref[0, g]

    def finalize(outs, half):
        return outs[0]

    _attn_pipeline(rc, 1, 2, get_qk, get_v, finalize, o_ref, s_buf, m_buf, e_buf)


def _softmax_scratch(n_sub, rc, s):
    return [pltpu.VMEM((2, n_sub, rc, s), F32),
            pltpu.VMEM((2, n_sub, rc, 1), F32),
            pltpu.VMEM((2, n_sub, rc, s), BF16)]


def _key_split_scratch(n_sub, rc, s):
    return (_softmax_scratch(n_sub, rc, s // 2)
            + [pltpu.VMEM((2, n_sub, rc, 1), F32),
               pltpu.VMEM((n_sub, rc, 2 * LANES), F32),
               pltpu.VMEM((n_sub, rc, 1), F32)])


def _attn_call(kernel, name, q, kt, v, l, extra, n_sub, rc, splits=1,
               make_scratch=_softmax_scratch):
    b, _, s, _ = q.shape
    rc = min(rc, s)

    def part_spec(a):
        if a.shape[1] < splits:
            return pl.BlockSpec((1,) + a.shape[1:], lambda bi, part: (bi, 0, 0, 0))
        return pl.BlockSpec((1, a.shape[1] // splits) + a.shape[2:],
                            lambda bi, part: (bi, part, 0, 0))

    return pl.pallas_call(
        functools.partial(kernel, rc=rc),
        grid=(b, splits),
        in_specs=[part_spec(q), part_spec(kt), part_spec(v)]
        + [_layer_spec(e, l) for e in extra],
        out_specs=pl.BlockSpec((1, OUT_GROUPS // splits, s, LANES),
                               lambda bi, part: (bi, part, 0, 0)),
        out_shape=jax.ShapeDtypeStruct((b, OUT_GROUPS, s, LANES), BF16),
        scratch_shapes=make_scratch(n_sub, rc, s),
        compiler_params=_cparams(2),
        name=name,
    )(q, kt, v, *extra)


def _merge_kernel(x_ref, g1_ref, ya_ref, yb_ref, yc_ref, wg_ref, wb_ref, wo_ref, o_ref):
    x = x_ref[...]
    h = _rms(x, g1_ref[...]).astype(BF16)
    merged = None
    for n, y_ref in enumerate((ya_ref, yb_ref, yc_ref)):
        logits = jnp.dot(h, wg_ref[:, n * D_MODEL:(n + 1) * D_MODEL],
                         preferred_element_type=F32)
        y = jnp.concatenate([y_ref[0, g] for g in range(OUT_GROUPS)], axis=1)
        z = jnp.dot(y, wb_ref[n], preferred_element_type=F32)
        term = jax.nn.sigmoid(logits) * z
        merged = term if merged is None else merged + term
    o_ref[...] = x + jnp.dot(merged.astype(BF16), wo_ref[...],
                             preferred_element_type=F32)


def _merge_call(x2, l, g1, ya, yb, yc, wg, wb, wo):
    n = x2.shape[0]
    tm = min(TM_MERGE, ya.shape[2])
    nt = ya.shape[2] // tm
    tok = pl.BlockSpec((tm, D_MODEL), lambda i: (i, 0))
    y_spec = pl.BlockSpec((1, OUT_GROUPS, tm, LANES), lambda i: (i // nt, 0, i % nt, 0))
    return pl.pallas_call(
        _merge_kernel,
        grid=(n // tm,),
        in_specs=[tok, _layer_spec(g1, l), y_spec, y_spec, y_spec,
                  _layer_spec(wg, l), _layer_spec(wb, l), _layer_spec(wo, l)],
        out_specs=tok,
        out_shape=jax.ShapeDtypeStruct((n, D_MODEL), F32),
        compiler_params=_cparams(1),
        name="gated_merge",
    )(x2, g1, ya, yb, yc, wg, wb, wo)


FF_CHUNK = 1024


def _mlp_kernel(x_ref, g2_ref, w1_ref, w2_ref, gf_ref, o_ref, *, final):
    x = x_ref[...]
    h = _rms(x, g2_ref[...]).astype(BF16)
    acc = x
    for c in range(D_FF // FF_CHUNK):
        cols = pl.ds(c * FF_CHUNK, FF_CHUNK)
        hid = jnp.dot(h, w1_ref[:, cols], preferred_element_type=F32)
        hid = jnp.square(jnp.maximum(hid, 0.0)).astype(BF16)
        acc = acc + jnp.dot(hid, w2_ref[cols, :], preferred_element_type=F32)
    if final:
        acc = _rms(acc, gf_ref[...])
    o_ref[...] = acc


def _mlp_call(x2, l, g2, w1, w2, gf, final):
    n = x2.shape[0]
    tm = TM_MLP
    tok = pl.BlockSpec((tm, D_MODEL), lambda i: (i, 0))
    return pl.pallas_call(
        functools.partial(_mlp_kernel, final=final),
        grid=(n // tm,),
        in_specs=[tok, _layer_spec(g2, l), _layer_spec(w1, l),
                  _layer_spec(w2, l), _const_spec(gf.shape)],
        out_specs=tok,
        out_shape=jax.ShapeDtypeStruct((n, D_MODEL), F32),
        compiler_params=_cparams(1),
        name="mlp_final" if final else "mlp",
    )(x2, g2, w1, w2, gf)


def _angles(pos, dim, theta):
    inv_freq = theta ** (-jnp.arange(0, dim, 2, dtype=F32) / dim)
    return pos.astype(F32)[:, None] * inv_freq[None, :]


def _rope_tables(seq):
    t = jnp.arange(seq)
    half = A_HEAD_DIM // 2
    row_ang = _angles(t // GRID_W, half, AXIAL_THETA)
    col_ang = _angles(t % GRID_W, half, AXIAL_THETA)
    b_ang = _angles(t, B_ROT_DIM, ROPE_THETA)
    c_ang = _angles(t, C_ROPE_DIM, MLA_THETA)

    def pair(ang):
        c, s = jnp.cos(ang), jnp.sin(ang)
        return jnp.concatenate([c, c], 1), jnp.concatenate([-s, s], 1)

    one = lambda w: jnp.ones((seq, w), F32)
    zero = lambda w: jnp.zeros((seq, w), F32)

    rc, rs = pair(row_ang)
    cc, cs = pair(col_ang)
    cos_a = jnp.concatenate([rc, cc] * 2, 1)
    sin_a = jnp.concatenate([rs, cs] * 2, 1)

    bc, bs = pair(b_ang)
    cos_b = jnp.concatenate([bc, one(B_QK_DIM - B_ROT_DIM)] * 4, 1)
    sin_b = jnp.concatenate([bs, zero(B_QK_DIM - B_ROT_DIM)] * 4, 1)

    mc, ms = pair(c_ang)
    cos_c = jnp.concatenate([one(64), mc, one(32)], 1)
    sin_c = jnp.concatenate([zero(64), ms, zero(32)], 1)
    return cos_a, sin_a, cos_b, sin_b, cos_c, sin_c


def _arrange_weights(w_in, c_w_uq, c_w_ukv, w_branch):
    depth = w_in.shape[0]
    order = jnp.array(A_HEAD_ORDER)
    a_q = w_in[:, :, 0:512].reshape(depth, D_MODEL, A_HEADS, A_HEAD_DIM)
    a_q = a_q[:, :, order, :].reshape(depth, D_MODEL, 512)
    c_kr = jnp.zeros((depth, D_MODEL, LANES), w_in.dtype).at[:, :, 64:96].set(
        w_in[:, :, 2944:2976])
    w_main = jnp.concatenate(
        [w_in[:, :, 2304:2944], c_kr, a_q, w_in[:, :, 512:640], w_in[:, :, 768:1792],
         w_in[:, :, 640:768], w_in[:, :, 1792:2304]], axis=2)
    assert w_main.shape[2] == PROJ_COLS
    w_gate = w_in[:, :, 2976:]

    uq = c_w_uq.reshape(depth, C_Q_RANK, C_HEADS, C_NOPE_DIM + C_ROPE_DIM)
    uq = jnp.pad(uq, ((0, 0), (0, 0), (0, 0), (0, LANES - C_NOPE_DIM - C_ROPE_DIM)))
    wuq = uq.reshape(depth, C_Q_RANK, C_HEADS * LANES)
    ukv = c_w_ukv.reshape(depth, C_KV_RANK, C_HEADS, C_NOPE_DIM + C_V_DIM)
    kn = jnp.pad(ukv[..., :C_NOPE_DIM], ((0, 0), (0, 0), (0, 0), (0, LANES - C_NOPE_DIM)))
    wkn = kn.reshape(depth, C_KV_RANK, C_HEADS * LANES)
    wv = ukv[..., C_NOPE_DIM:].reshape(depth, C_KV_RANK, C_HEADS * C_V_DIM)

    wb_a = w_branch[:, 0].reshape(depth, A_HEADS, A_HEAD_DIM, D_MODEL)[:, order]
    wb = jnp.stack([wb_a.reshape(depth, BRANCH_W, D_MODEL),
                    w_branch[:, 1], w_branch[:, 2]], axis=1)
    return w_main, w_gate, wuq, wkn, wv, wb


def kernel(x, ln1_g, w_in, a_q_norm, a_k_norm, b_lambda, b_subln, c_q_norm, c_kv_norm,
           c_w_uq, c_w_ukv, w_branch, w_out, ln2_g, w_ff1, w_ff2, final_g):
    b, s, d = x.shape
    n = b * s
    tables = _rope_tables(s)
    gidx = jnp.arange(BRANCH_W) // A_HEAD_DIM
    gmat = (gidx[:, None] == gidx[None, :]).astype(BF16)
    gf = final_g.reshape(1, d)
    w_in, c_w_uq, c_w_ukv, w_branch, w_out, w_ff1, w_ff2 = (
        t.astype(BF16) for t in (w_in, c_w_uq, c_w_ukv, w_branch, w_out, w_ff1, w_ff2))

    depth = w_in.shape[0]
    w_main, w_gate, wuq, wkn, wv, wb = _arrange_weights(w_in, c_w_uq, c_w_ukv, w_branch)
    row = lambda t: t.reshape(depth, 1, t.shape[-1])
    g1, g2 = row(ln1_g), row(ln2_g)
    proj_params = (g1, w_main, row(jnp.tile(a_q_norm, (1, A_HEADS))),
                   row(jnp.tile(a_k_norm, (1, A_KV_HEADS))), row(c_q_norm), row(c_kv_norm),
                   wuq, wkn, wv)
    lambda_init = jnp.array([0.8 - 0.6 * math.exp(-0.3 * l) for l in range(depth)], F32)
    diff_params = (b_lambda, jnp.broadcast_to(lambda_init[:, None, None], (depth, 1, LANES)),
                   row(jnp.tile(b_subln, (1, 2))))

    x2 = x.reshape(n, d)
    for l in range(depth):
        qa, ka, va, qb, kb, vb, qc, kc, vc = _proj_call(x2, l, proj_params, gmat, tables, b, s)
        ya = _attn_call(_attn_a_kernel, "attn_gqa", qa, ka, va, l, (), 2, RC_GQA, splits=2,
                        make_scratch=_key_split_scratch)
        yb = _attn_call(_attn_b_kernel, "attn_diff", qb, kb, vb, l, diff_params, 2, RC_DIFF)
        yc = _attn_call(_attn_c_kernel, "attn_mla", qc, kc, vc, l, (), 1, RC_MLA, splits=2)
        x2 = _merge_call(x2, l, g1, ya, yb, yc, w_gate, wb, w_out)
        x2 = _mlp_call(x2, l, g2, w_ff1, w_ff2, gf, l == depth - 1)
    return x2.reshape(b, s, d)
```
